```python
import jax
import jax.numpy as jnp
from jax import lax
import numpy as np

D_MODEL = 2048
BATCH = 4
SEQ = 8192
DEPTH = 4

N_META = 16
GRID_W = 64
GLA_HEADS = 4
GLA_DK = D_MODEL // 2 // GLA_HEADS
GLA_DV = D_MODEL // GLA_HEADS
GATE_RANK = 16
GATE_NORMALIZER = 16.0
CHUNK = 64
NA_HEAD_DIM = 64
NA_HEADS = D_MODEL // 2 // NA_HEAD_DIM
WIN_H = 8
WIN_W = 16
N_EXPERTS = 32
TOP_K = 4
EXPERT_FF = 768
SWIGLU_ALPHA = 1.702
SWIGLU_LIMIT = 7.0
GROUP_ROWS = 256
LN_EPS = 1e-5
RMS_EPS = 1e-6
DEEPNORM_ALPHA = (2 * DEPTH) ** 0.25
DEEPNORM_BETA = (8 * DEPTH) ** -0.25
IN_SPLITS = (
    GLA_HEADS * GLA_DK,
    GLA_HEADS * GLA_DK,
    GLA_HEADS * GLA_DV,
    GLA_HEADS * GLA_DV,
    GATE_RANK,
    GATE_RANK,
    NA_HEADS * NA_HEAD_DIM,
    NA_HEADS * NA_HEAD_DIM,
    NA_HEADS * NA_HEAD_DIM,
    D_MODEL,
    D_MODEL,
)
IN_WIDTH = sum(IN_SPLITS)

kernel_name = 'hybrid_gla_natten_moe_encoder'


def layer_norm(x, w, b):
    xf = x.astype(jnp.float32)
    mu = jnp.mean(xf, axis=-1, keepdims=True)
    var = jnp.mean(jnp.square(xf - mu), axis=-1, keepdims=True)
    y = (xf - mu) * lax.rsqrt(var + LN_EPS)
    return (y * w.astype(jnp.float32) + b.astype(jnp.float32)).astype(x.dtype)


def gla_one_direction(q, k, v, g):
    bsz, length, heads, dk = q.shape
    dv = v.shape[-1]
    n_chunks = length // CHUNK

    def chunks(t):
        return t.reshape(bsz, n_chunks, CHUNK, heads, t.shape[-1]).transpose(1, 0, 3, 2, 4)

    lower = jnp.tril(jnp.ones((CHUNK, CHUNK), dtype=bool))

    def step(state, inp):
        qc, kc, vc, gc = inp
        b = jnp.cumsum(gc, axis=2)
        o_inter = jnp.einsum('bhcd,bhde->bhce', qc * jnp.exp(b), state)
        rel = jnp.where(lower[:, :, None], b[:, :, :, None, :] - b[:, :, None, :, :], -jnp.inf)
        scores = jnp.einsum('bhid,bhjd,bhijd->bhij', qc, kc, jnp.exp(rel))
        o = o_inter + jnp.einsum('bhij,bhje->bhie', scores, vc)
        b_last = b[:, :, -1, :]
        k_dec = kc * jnp.exp(b_last[:, :, None, :] - b)
        state = jnp.exp(b_last)[..., None] * state + jnp.einsum('bhjd,bhje->bhde', k_dec, vc)
        return state, o

    state0 = jnp.zeros((bsz, heads, dk, dv), jnp.float32)
    _, o = lax.scan(step, state0, (chunks(q), chunks(k), chunks(v), chunks(g)))
    return o.transpose(1, 0, 3, 2, 4).reshape(bsz, length, heads, dv)


def bidirectional_gla(q, k, v, g_fwd, g_bwd):
    pad = (-N_META) % CHUNK
    widths = ((0, 0), (pad, 0), (0, 0), (0, 0))
    q, k, v, g_fwd, g_bwd = [jnp.pad(t.astype(jnp.float32), widths) for t in (q, k, v, g_fwd, g_bwd)]
    o_f = gla_one_direction(q, k, v, g_fwd)
    flip = lambda t: t[:, ::-1]
    o_b = flip(gla_one_direction(flip(q), flip(k), flip(v), flip(g_bwd)))
    return (o_f + o_b)[:, pad:]


def neighborhood_attention(q, k, v, rpb, meta_bias):
    bsz, length, heads, dh = q.shape
    n_tok = length - N_META
    rows = n_tok // GRID_W
    kh = min(WIN_H, rows)
    q = q * (dh ** -0.5)
    qm, km, vm = q[:, :N_META], k[:, :N_META], v[:, :N_META]
    grid = lambda t: t[:, N_META:].reshape(bsz, rows, GRID_W, heads, dh)
    qg, kg, vg = grid(q), grid(k), grid(v)
    cols = np.arange(GRID_W)
    col_start = np.clip(cols - WIN_W // 2, 0, GRID_W - WIN_W)
    col_idx = col_start[:, None] + np.arange(WIN_W)[None, :]
    dc = col_idx - cols[:, None] + (WIN_W - 1)
    rpb_c = rpb[:, :, dc]
    mb = meta_bias.astype(jnp.float32)[None, :, None, :]
    n_loc = kh * WIN_W

    def row_block(r):
        r0 = jnp.clip(r - kh // 2, 0, rows - kh)
        k_rows = lax.dynamic_slice_in_dim(kg, r0, kh, axis=1)
        v_rows = lax.dynamic_slice_in_dim(vg, r0, kh, axis=1)
        k_nb = k_rows[:, :, col_idx]
        v_nb = v_rows[:, :, col_idx]
        q_row = lax.dynamic_index_in_dim(qg, r, axis=1, keepdims=False)
        dr = r0 + jnp.arange(kh) - r + (WIN_H - 1)
        bias = jnp.transpose(rpb_c[:, dr], (0, 2, 1, 3))
        s_loc = jnp.einsum('bqhd,bkqwhd->bhqkw', q_row, k_nb).astype(jnp.float32) + bias[None]
        s_meta = jnp.einsum('bqhd,bmhd->bhqm', q_row, km).astype(jnp.float32) + mb
        s = jnp.concatenate([s_loc.reshape(bsz, heads, GRID_W, n_loc), s_meta], axis=-1)
        p = jax.nn.softmax(s, axis=-1).astype(v.dtype)
        p_loc = p[..., :n_loc].reshape(bsz, heads, GRID_W, kh, WIN_W)
        return (jnp.einsum('bhqkw,bkqwhd->bqhd', p_loc, v_nb)
                + jnp.einsum('bhqm,bmhd->bqhd', p[..., n_loc:], vm))

    o_grid = lax.map(row_block, jnp.arange(rows))
    o_grid = jnp.transpose(o_grid, (1, 0, 2, 3, 4)).reshape(bsz, n_tok, heads, dh)
    s_mm = jnp.einsum('bqhd,bmhd->bhqm', qm, km).astype(jnp.float32) + mb
    p_mm = jax.nn.softmax(s_mm, axis=-1).astype(v.dtype)
    o_meta = jnp.einsum('bhqm,bmhd->bqhd', p_mm, vm)
    return jnp.concatenate([o_meta, o_grid], axis=1)


def hybrid_mixer(u, w_in, gk_fwd_w, gk_fwd_b, gk_bwd_w, gk_bwd_b, gla_norm_w, w_branch_a,
                 rpb, meta_bias, w_branch_b, w_out):
    bsz, length, _ = u.shape
    proj = u @ w_in
    split_points = np.cumsum(IN_SPLITS)[:-1].tolist()
    (q_a, k_a, v_a, r_a, lr_f, lr_b, q_b, k_b, v_b, gate_a, gate_b) = jnp.split(proj, split_points, axis=-1)
    heads = lambda t, h: t.reshape(bsz, length, h, -1)
    g_f = jax.nn.log_sigmoid((lr_f @ gk_fwd_w + gk_fwd_b).astype(jnp.float32)) / GATE_NORMALIZER
    g_b = jax.nn.log_sigmoid((lr_b @ gk_bwd_w + gk_bwd_b).astype(jnp.float32)) / GATE_NORMALIZER
    o_a = bidirectional_gla(heads(q_a, GLA_HEADS) * (GLA_DK ** -0.5), heads(k_a, GLA_HEADS),
                            heads(v_a, GLA_HEADS), heads(g_f, GLA_HEADS), heads(g_b, GLA_HEADS))
    o_a = o_a * lax.rsqrt(jnp.mean(jnp.square(o_a), axis=-1, keepdims=True) + RMS_EPS)
    o_a = (o_a * gla_norm_w.astype(jnp.float32)).astype(u.dtype) * jax.nn.silu(heads(r_a, GLA_HEADS))
    y_a = o_a.reshape(bsz, length, -1) @ w_branch_a
    o_b = neighborhood_attention(heads(q_b, NA_HEADS), heads(k_b, NA_HEADS), heads(v_b, NA_HEADS),
                                 rpb, meta_bias)
    y_b = o_b.reshape(bsz, length, -1) @ w_branch_b
    mixed = jax.nn.sigmoid(gate_a) * y_a + jax.nn.sigmoid(gate_b) * y_b
    return mixed @ w_out


def moe(x, router_w, router_b, w1, b1, w2, b2):
    bsz, length, d = x.shape
    n = bsz * length
    xf = x.reshape(n, d)
    logits = (xf @ router_w + router_b).astype(jnp.float32)
    top_v, top_e = lax.top_k(logits, TOP_K)
    gate = jax.nn.softmax(top_v, axis=-1)
    flat_e = top_e.reshape(-1)
    flat_tok = jnp.arange(n * TOP_K, dtype=jnp.int32) // TOP_K
    order = jnp.argsort(flat_e)
    e_sorted = flat_e[order]
    tok_sorted = flat_tok[order]
    w_sorted = gate.reshape(-1)[order]
    counts = jnp.bincount(flat_e, length=N_EXPERTS)
    padded = (counts + GROUP_ROWS - 1) // GROUP_ROWS * GROUP_ROWS
    start = jnp.cumsum(counts) - counts
    pend = jnp.cumsum(padded)
    pstart = pend - padded
    dest = pstart[e_sorted] + (jnp.arange(n * TOP_K) - start[e_sorted])
    n_groups = -(-(n * TOP_K) // GROUP_ROWS) + N_EXPERTS
    n_rows = n_groups * GROUP_ROWS
    row_tok = jnp.zeros((n_rows,), jnp.int32).at[dest].set(tok_sorted)
    row_w = jnp.zeros((n_rows,), jnp.float32).at[dest].set(w_sorted)
    group_e = jnp.clip(jnp.searchsorted(pend, jnp.arange(n_groups) * GROUP_ROWS, side='right'),
                       0, N_EXPERTS - 1)

    def expert_group(args):
        tok_g, w_g, e = args
        h = xf[tok_g] @ w1[e] + b1[e]
        x_glu = jnp.minimum(h[:, 0::2], SWIGLU_LIMIT)
        x_lin = jnp.clip(h[:, 1::2], -SWIGLU_LIMIT, SWIGLU_LIMIT)
        act = x_glu * jax.nn.sigmoid(SWIGLU_ALPHA * x_glu) * (x_lin + 1.0)
        y = act @ w2[e] + b2[e]
        return y * w_g[:, None].astype(y.dtype)

    y_rows = lax.map(expert_group, (row_tok.reshape(n_groups, GROUP_ROWS),
                                    row_w.reshape(n_groups, GROUP_ROWS), group_e))
    out = jax.ops.segment_sum(y_rows.reshape(n_rows, d), row_tok, num_segments=n)
    return out.reshape(bsz, length, d)


def _normal(key, shape, scale):
    return jax.random.normal(key, shape, jnp.float32) * scale


def setup_inputs(seed: int = 0) -> dict:
    key = jax.random.key(seed)
    ks = jax.random.split(key, 25)
    L, D, E, F = DEPTH, D_MODEL, N_EXPERTS, EXPERT_FF
    kdim = GLA_HEADS * GLA_DK
    vdim = GLA_HEADS * GLA_DV
    ndim = NA_HEADS * NA_HEAD_DIM
    return {
        'x': _normal(ks[0], (BATCH, SEQ, D), 1.0),
        'meta_tokens': _normal(ks[1], (N_META, D), 1.0),
        'ln0_w': 1.0 + _normal(ks[2], (D,), 0.01),
        'ln0_b': _normal(ks[3], (D,), 0.01),
        'w_in': _normal(ks[4], (L, D, IN_WIDTH), D ** -0.5),
        'gk_fwd_w': _normal(ks[5], (L, GATE_RANK, kdim), GATE_RANK ** -0.5),
        'gk_fwd_b': _normal(ks[6], (L, kdim), 0.1),
        'gk_bwd_w': _normal(ks[7], (L, GATE_RANK, kdim), GATE_RANK ** -0.5),
        'gk_bwd_b': _normal(ks[8], (L, kdim), 0.1),
        'gla_norm_w': 1.0 + _normal(ks[9], (L, GLA_DV), 0.01),
        'w_branch_a': _normal(ks[10], (L, vdim, D), vdim ** -0.5),
        'rpb': _normal(ks[11], (L, NA_HEADS, 2 * WIN_H - 1, 2 * WIN_W - 1), 0.1),
        'meta_bias': _normal(ks[12], (L, NA_HEADS, N_META), 0.1),
        'w_branch_b': _normal(ks[13], (L, ndim, D), ndim ** -0.5),
        'w_out': _normal(ks[14], (L, D, D), DEEPNORM_BETA * D ** -0.5),
        'ln1_w': 1.0 + _normal(ks[15], (L, D), 0.01),
        'ln1_b': _normal(ks[16], (L, D), 0.01),
        'router_w': _normal(ks[17], (L, D, E), D ** -0.5),
        'router_b': _normal(ks[18], (L, E), 0.01),
        'w1': _normal(ks[19], (L, E, D, 2 * F), D ** -0.5),
        'b1': _normal(ks[20], (L, E, 2 * F), 0.01),
        'w2': _normal(ks[21], (L, E, F, D), DEEPNORM_BETA * F ** -0.5),
        'b2': _normal(ks[22], (L, E, D), 0.01),
        'ln2_w': 1.0 + _normal(ks[23], (L, D), 0.01),
        'ln2_b': _normal(ks[24], (L, D), 0.01),
    }


def reference(x, meta_tokens, ln0_w, ln0_b, w_in, gk_fwd_w, gk_fwd_b, gk_bwd_w, gk_bwd_b,
              gla_norm_w, w_branch_a, rpb, meta_bias, w_branch_b, w_out, ln1_w, ln1_b,
              router_w, router_b, w1, b1, w2, b2, ln2_w, ln2_b):
    bsz = x.shape[0]
    meta = jnp.broadcast_to(meta_tokens.astype(x.dtype)[None], (bsz, N_META, x.shape[-1]))
    h = layer_norm(jnp.concatenate([meta, x], axis=1), ln0_w, ln0_b)
    for l in range(DEPTH):
        mix = hybrid_mixer(h, w_in[l], gk_fwd_w[l], gk_fwd_b[l], gk_bwd_w[l], gk_bwd_b[l],
                           gla_norm_w[l], w_branch_a[l], rpb[l], meta_bias[l], w_branch_b[l], w_out[l])
        h = layer_norm(DEEPNORM_ALPHA * h + mix, ln1_w[l], ln1_b[l])
        ffn = moe(h, router_w[l], router_b[l], w1[l], b1[l], w2[l], b2[l])
        h = layer_norm(DEEPNORM_ALPHA * h + ffn, ln2_w[l], ln2_b[l])
    return h[:, N_META:]
```

```python
import functools

import numpy as np
import jax
import jax.numpy as jnp
from jax import lax
from jax.experimental import pallas as pl
from jax.experimental.pallas import tpu as pltpu

N_META = 16
GRID_W = 64
GLA_HEADS = 4
GATE_RANK = 16
GATE_NORMALIZER = 16.0
CHUNK = 64
NA_HEAD_DIM = 64
WIN_H = 8
WIN_W = 16
TOP_K = 4
SWIGLU_ALPHA = 1.702
SWIGLU_LIMIT = 7.0
GROUP_ROWS = 256
LN_EPS = 1e-5
RMS_EPS = 1e-6

FRONT = CHUNK
PAD = FRONT - N_META
SMALL_N = 128
NEG = -1e30
VMEM_LIMIT = 52 * 1024 * 1024
HIGHEST = lax.Precision.HIGHEST
NT = (((1,), (1,)), ((), ()))
TN = (((0,), (0,)), ((), ()))

f32 = jnp.float32
bf16 = jnp.bfloat16


def _params(*sem):
    return pltpu.CompilerParams(dimension_semantics=sem, vmem_limit_bytes=VMEM_LIMIT)


def _tile(n, prefs):
    for p in prefs:
        if n % p == 0:
            return p
    raise ValueError(f"no tile in {prefs} divides {n}")


def _ln_kernel(*refs, alpha, has_res, has_proj):
    it = iter(refs)
    x_ref = next(it)
    res_ref = next(it) if has_res else None
    w_ref, b_ref = next(it), next(it)
    pw_ref, pb_ref = (next(it), next(it)) if has_proj else (None, None)
    h_ref, hb_ref = next(it), next(it)
    z = x_ref[...]
    if has_res:
        z = alpha * res_ref[...] + z
    mu = jnp.mean(z, axis=-1, keepdims=True)
    d = z - mu
    var = jnp.mean(d * d, axis=-1, keepdims=True)
    h = d * lax.rsqrt(var + LN_EPS) * w_ref[...] + b_ref[...]
    h_ref[...] = h
    hb_ref[...] = h.astype(bf16)
    if has_proj:
        p_ref = next(it)
        p_ref[...] = jnp.dot(h, pw_ref[...], precision=HIGHEST, preferred_element_type=f32) + pb_ref[...]


def _layer_norm(x, res, alpha, w, b, pw=None, pb=None):
    t, d = x.shape
    tm = _tile(t, (256, 128, 64))
    has_res, has_proj = res is not None, pw is not None
    row = pl.BlockSpec((tm, d), lambda i: (i, 0))
    vec = pl.BlockSpec((1, d), lambda i: (0, 0))
    ins, specs = [x], [row]
    if has_res:
        ins.append(res)
        specs.append(row)
    ins += [w.reshape(1, d), b.reshape(1, d)]
    specs += [vec, vec]
    outs = [jax.ShapeDtypeStruct((t, d), f32), jax.ShapeDtypeStruct((t, d), bf16)]
    out_specs = [row, row]
    if has_proj:
        ins += [pw, pb.reshape(1, SMALL_N)]
        specs += [pl.BlockSpec((d, SMALL_N), lambda i: (0, 0)), pl.BlockSpec((1, SMALL_N), lambda i: (0, 0))]
        outs.append(jax.ShapeDtypeStruct((t, SMALL_N), f32))
        out_specs.append(pl.BlockSpec((tm, SMALL_N), lambda i: (i, 0)))
    return pl.pallas_call(
        functools.partial(_ln_kernel, alpha=alpha, has_res=has_res, has_proj=has_proj),
        grid=(t // tm,), in_specs=specs, out_specs=out_specs, out_shape=outs,
        compiler_params=_params("parallel"), name="layer_norm",
    )(*ins)


def _mm_kernel(x_ref, w_ref, o_ref):
    o_ref[...] = jnp.dot(x_ref[...], w_ref[...], preferred_element_type=f32).astype(o_ref.dtype)


def _matmul(x, w, out_dtype, name):
    m, k = x.shape
    n = w.shape[1]
    tm = _tile(m, (768, 512, 384, 256, 128))
    tn = _tile(n, (1024, 512, 256, 128))
    return pl.pallas_call(
        _mm_kernel,
        grid=(n // tn, m // tm),
        in_specs=[pl.BlockSpec((tm, k), lambda j, i: (i, 0)), pl.BlockSpec((k, tn), lambda j, i: (0, j))],
        out_specs=pl.BlockSpec((tm, tn), lambda j, i: (i, j)),
        out_shape=jax.ShapeDtypeStruct((m, n), out_dtype),
        compiler_params=_params("parallel", "parallel"), name=name,
    )(x, w)


def _branch_kernel(oa_ref, ob_ref, wa_ref, wb_ref, ga_ref, gb_ref, o_ref):
    ya = jnp.dot(oa_ref[...], wa_ref[...], preferred_element_type=f32)
    yb = jnp.dot(ob_ref[...], wb_ref[...], preferred_element_type=f32)
    mixed = jax.nn.sigmoid(ga_ref[...].astype(f32)) * ya + jax.nn.sigmoid(gb_ref[...].astype(f32)) * yb
    o_ref[...] = mixed.astype(o_ref.dtype)


def _branch_merge(o_a, o_b, w_a, w_b, proj, ga_col, gb_col):
    t, ka = o_a.shape
    kb = o_b.shape[1]
    d = w_a.shape[1]
    tm = _tile(t, (768, 512, 384, 256, 128))
    tn = 512
    assert ga_col % tn == 0 and gb_col % tn == 0 and d % tn == 0
    return pl.pallas_call(
        _branch_kernel,
        grid=(d // tn, t // tm),
        in_specs=[
            pl.BlockSpec((tm, ka), lambda j, i: (i, 0)),
            pl.BlockSpec((tm, kb), lambda j, i: (i, 0)),
            pl.BlockSpec((ka, tn), lambda j, i: (0, j)),
            pl.BlockSpec((kb, tn), lambda j, i: (0, j)),
            pl.BlockSpec((tm, tn), lambda j, i: (i, ga_col // tn + j)),
            pl.BlockSpec((tm, tn), lambda j, i: (i, gb_col // tn + j)),
        ],
        out_specs=pl.BlockSpec((tm, tn), lambda j, i: (i, j)),
        out_shape=jax.ShapeDtypeStruct((t, d), bf16),
        compiler_params=_params("parallel", "parallel"), name="branch_merge",
    )(o_a, o_b, w_a, w_b, proj, proj)


N_LEVELS = 6
DIAG_LEVEL = N_LEVELS + 1
KDEC_BLOCK = N_LEVELS + 1


def _gla_constants(reverse):
    c = CHUNK
    i = np.arange(c)[:, None]
    s = np.arange(c)[None, :]
    blocks = [(s >= i) if reverse else (s <= i)]
    level = np.zeros((c, c), np.int32)
    size = c
    for lv in range(1, N_LEVELS + 1):
        half = size // 2
        m = np.zeros((c, c), bool)
        for r in range(c):
            base = (r // size) * size
            first_half = (r % size) < half
            if not reverse:
                mid = base + half - 1
                if first_half:
                    m[r, r + 1:mid + 1] = True
                else:
                    m[r, mid + 1:r + 1] = True
            else:
                mid = base + half
                if first_half:
                    m[r, r:mid] = True
                else:
                    m[r, mid:r] = True
        blocks.append(m)
        same = (i // size) == (s // size)
        if not reverse:
            cond = same & ((i % size) >= half) & ((s % size) < half)
        else:
            cond = same & ((i % size) < half) & ((s % size) >= half)
        level[cond] = lv
        size = half
    level[np.eye(c, dtype=bool)] = DIAG_LEVEL
    blocks.append((s < i) if reverse else (s > i))
    return np.concatenate(blocks, axis=0).astype(np.float32), level


def _gla_kernel(*refs, reverse, cps, final, dk):
    it = iter(refs)
    q_ref, k_ref, v_ref, lr_ref, gw_ref, gb_ref, ms_ref, lvl_ref = (next(it) for _ in range(8))
    if final:
        r_ref, ob_ref, nw_ref = next(it), next(it), next(it)
    o_ref = next(it)
    st_ref = next(it)
    step = pl.program_id(2)
    nsteps = pl.num_programs(2)

    @pl.when(step == 0)
    def _():
        st_ref[...] = jnp.zeros_like(st_ref)

    blk = (nsteps - 1 - step) if reverse else step
    ms = ms_ref[...]
    lvl = lvl_ref[...]
    row = lax.broadcasted_iota(jnp.int32, (CHUNK, 1), 0)
    scale = dk ** -0.5

    def chunk_body(ci, carry):
        c = (cps - 1 - ci) if reverse else ci
        r0 = pl.multiple_of(c * CHUNK, CHUNK)
        rows = pl.ds(r0, CHUNK)
        valid = (row >= PAD) | (blk * cps + c > 0)
        q = jnp.where(valid, q_ref[rows, :].astype(f32) * scale, 0.0)
        k = jnp.where(valid, k_ref[rows, :].astype(f32), 0.0)
        v = jnp.where(valid, v_ref[rows, :], jnp.zeros((), bf16))
        x = jnp.dot(lr_ref[rows, :], gw_ref[...], precision=HIGHEST, preferred_element_type=f32) + gb_ref[...]
        g = (jnp.minimum(x, 0.0) - jnp.log1p(jnp.exp(-jnp.abs(x)))) * (1.0 / GATE_NORMALIZER)
        g = jnp.where(valid, g, 0.0)
        g_hi = g.astype(bf16)
        g_lo = (g - g_hi.astype(f32)).astype(bf16)
        e = (jnp.dot(ms, g_hi, preferred_element_type=f32) + jnp.dot(ms, g_lo, preferred_element_type=f32))
        cum = e[0:CHUNK]
        scores = jnp.zeros((CHUNK, CHUNK), f32)
        for lv in range(1, N_LEVELS + 1):
            w = jnp.exp(e[lv * CHUNK:(lv + 1) * CHUNK])
            a = lax.dot_general((q * w).astype(bf16), (k * w).astype(bf16), NT, preferred_element_type=f32)
            scores = jnp.where(lvl == lv, a, scores)
        a = lax.dot_general(q.astype(bf16), k.astype(bf16), NT, preferred_element_type=f32)
        scores = jnp.where(lvl == DIAG_LEVEL, a, scores)

        st = st_ref[...]
        o = lax.dot_general((q * jnp.exp(cum)).astype(bf16), st.astype(bf16), NT, preferred_element_type=f32)
        o = o + jnp.dot(scores.astype(bf16), v, preferred_element_type=f32)
        kd = (k * jnp.exp(e[KDEC_BLOCK * CHUNK:(KDEC_BLOCK + 1) * CHUNK])).astype(bf16)
        total = jnp.exp(cum[0:1] if reverse else cum[CHUNK - 1:CHUNK])
        st_ref[...] = st * total + lax.dot_general(v, kd, TN, preferred_element_type=f32)

        if final:
            o = o + ob_ref[rows, :]
            o = o * lax.rsqrt(jnp.mean(o * o, axis=-1, keepdims=True) + RMS_EPS) * nw_ref[...]
            o = o * jax.nn.silu(r_ref[rows, :].astype(f32))
        o_ref[rows, :] = o.astype(o_ref.dtype)
        return carry

    lax.fori_loop(0, cps, chunk_body, 0)


def _gla_direction(proj, lr, gw, gb, bsz, lp, reverse, final_args=None):
    t = proj.shape[0]
    heads = GLA_HEADS
    dk = gw.shape[1] // heads
    dv = 2 * dk
    n_chunks = lp // CHUNK
    cps = _tile(n_chunks, (3, 2, 1))
    rb = cps * CHUNK
    nsteps = n_chunks // cps
    final = final_args is not None
    ms, lvl = _gla_constants(reverse)
    nblk = ms.shape[0]

    def rblk(b, s):
        return b * nsteps + ((nsteps - 1 - s) if reverse else s)

    k_off = heads
    v_off = heads
    r_off = 2 * heads
    in_specs = [
        pl.BlockSpec((rb, dk), lambda b, h, s: (rblk(b, s), h)),
        pl.BlockSpec((rb, dk), lambda b, h, s: (rblk(b, s), k_off + h)),
        pl.BlockSpec((rb, dv), lambda b, h, s: (rblk(b, s), v_off + h)),
        pl.BlockSpec((rb, GATE_RANK), lambda b, h, s: (rblk(b, s), 0)),
        pl.BlockSpec((GATE_RANK, dk), lambda b, h, s: (0, h)),
        pl.BlockSpec((1, dk), lambda b, h, s: (0, h)),
        pl.BlockSpec((nblk, CHUNK), lambda b, h, s: (0, 0)),
        pl.BlockSpec((CHUNK, CHUNK), lambda b, h, s: (0, 0)),
    ]
    ins = [proj, proj, proj, lr, gw, gb.reshape(1, -1), jnp.asarray(ms, bf16), jnp.asarray(lvl)]
    if final:
        o_other, nw = final_args
        in_specs += [
            pl.BlockSpec((rb, dv), lambda b, h, s: (rblk(b, s), r_off + h)),
            pl.BlockSpec((rb, dv), lambda b, h, s: (rblk(b, s), h)),
            pl.BlockSpec((1, dv), lambda b, h, s: (0, 0)),
        ]
        ins += [proj, o_other, nw.reshape(1, dv)]
    return pl.pallas_call(
        functools.partial(_gla_kernel, reverse=reverse, cps=cps, final=final, dk=dk),
        grid=(bsz, heads, nsteps),
        in_specs=in_specs,
        out_specs=pl.BlockSpec((rb, dv), lambda b, h, s: (rblk(b, s), h)),
        out_shape=jax.ShapeDtypeStruct((t, heads * dv), bf16 if final else f32),
        scratch_shapes=[pltpu.VMEM((dv, dk), f32)],
        compiler_params=_params("parallel", "parallel", "arbitrary"),
        name="gla_final" if final else "gla_partial",
    )(*ins)


N_VARIANTS = WIN_H


def _na_bias_tables(rpb, grid_rows):
    del grid_rows
    n_layers, heads = rpb.shape[0], rpb.shape[1]
    cols = np.arange(GRID_W)
    col_start = np.clip(cols - WIN_W // 2, 0, GRID_W - WIN_W)
    kc = np.arange(GRID_W)
    in_win = (kc[None, :] >= col_start[:, None]) & (kc[None, :] < col_start[:, None] + WIN_W)
    dc = np.clip(kc[None, :] - cols[:, None] + (WIN_W - 1), 0, 2 * WIN_W - 2)
    dr = np.arange(N_VARIANTS)[:, None] - 3 + np.arange(WIN_H)[None, :] + (WIN_H - 1) - WIN_H // 2
    tbl = rpb[:, :, dr[:, None, :, None], dc[None, :, None, :]]
    tbl = jnp.where(in_win[None, None, None, :, None, :], tbl, NEG)
    tbl = tbl.reshape(n_layers, heads // 2, 2, N_VARIANTS, GRID_W, WIN_H * GRID_W)
    tbl = jnp.transpose(tbl, (0, 1, 3, 2, 4, 5))
    return tbl.reshape(n_layers, heads // 2, N_VARIANTS, 2 * GRID_W, WIN_H * GRID_W)


def _na_kernel(q_ref, k_ref, v_ref, bias_ref, mb_ref, o_ref, *, bps, grid_rows):
    step = pl.program_id(2)
    lane = lax.broadcasted_iota(jnp.int32, (1, 2 * NA_HEAD_DIM), 1)
    head0 = lane < NA_HEAD_DIM
    km = k_ref[PAD:FRONT, :]
    vm = v_ref[PAD:FRONT, :]
    mb = mb_ref[0]
    mbias = jnp.concatenate([jnp.broadcast_to(mb[0:1], (GRID_W, N_META)),
                             jnp.broadcast_to(mb[1:2], (GRID_W, N_META))], axis=0)
    zero = jnp.zeros((), bf16)
    scale = NA_HEAD_DIM ** -0.5

    def stacked_queries(j):
        q2 = q_ref[j * GRID_W:(j + 1) * GRID_W, :] * jnp.asarray(scale, bf16)
        return jnp.concatenate([jnp.where(head0, q2, zero), jnp.where(head0, zero, q2)], axis=0)

    def unstack(o):
        return jnp.where(head0, o[0:GRID_W], o[GRID_W:2 * GRID_W])

    def front_block(j):
        qq = stacked_queries(j)
        sm = lax.dot_general(qq, km, NT, preferred_element_type=f32) + mbias
        p = jnp.exp(sm - jnp.max(sm, axis=-1, keepdims=True))
        o = jnp.dot(p.astype(bf16), vm, preferred_element_type=f32) / jnp.sum(p, axis=-1, keepdims=True)
        row = lax.broadcasted_iota(jnp.int32, (GRID_W, 1), 0)
        o_ref[j * GRID_W:(j + 1) * GRID_W, :] = jnp.where(row >= PAD, unstack(o), 0.0).astype(o_ref.dtype)

    def grid_block(j):
        r = step * bps + j - 1
        r0 = jnp.clip(r - WIN_H // 2, 0, grid_rows - WIN_H)
        vi = r0 - r + WIN_H // 2 + 3
        ks = pl.multiple_of((r0 + 1) * GRID_W, GRID_W)
        kk = k_ref[pl.ds(ks, WIN_H * GRID_W), :]
        vv = v_ref[pl.ds(ks, WIN_H * GRID_W), :]
        qq = stacked_queries(j)
        sl = lax.dot_general(qq, kk, NT, preferred_element_type=f32) + bias_ref[0, vi]
        sm = lax.dot_general(qq, km, NT, preferred_element_type=f32) + mbias
        mx = jnp.maximum(jnp.max(sl, axis=-1, keepdims=True), jnp.max(sm, axis=-1, keepdims=True))
        pl_ = jnp.exp(sl - mx)
        pm = jnp.exp(sm - mx)
        den = jnp.sum(pl_, axis=-1, keepdims=True) + jnp.sum(pm, axis=-1, keepdims=True)
        o = jnp.dot(pl_.astype(bf16), vv, preferred_element_type=f32) + jnp.dot(pm.astype(bf16), vm, preferred_element_type=f32)
        o_ref[j * GRID_W:(j + 1) * GRID_W, :] = unstack(o / den).astype(o_ref.dtype)

    for j in range(bps):
        if j == 0:
            pl.when(step == 0)(functools.partial(front_block, j))
            pl.when(step > 0)(functools.partial(grid_block, j))
        else:
            grid_block(j)


def _neighborhood_attention(proj, bias_tbl, meta_bias, bsz, lp, q_col):
    t = proj.shape[0]
    n_pairs = bias_tbl.shape[0]
    width = 2 * NA_HEAD_DIM
    grid_rows = (lp - FRONT) // GRID_W
    assert grid_rows >= WIN_H
    n_blocks = lp // GRID_W
    bps = _tile(n_blocks, (3, 2, 1))
    nsteps = n_blocks // bps
    qb = q_col // width
    return pl.pallas_call(
        functools.partial(_na_kernel, bps=bps, grid_rows=grid_rows),
        grid=(bsz, n_pairs, nsteps),
        in_specs=[
            pl.BlockSpec((bps * GRID_W, width), lambda b, p, s: (b * nsteps + s, qb + p)),
            pl.BlockSpec((lp, width), lambda b, p, s: (b, qb + n_pairs + p)),
            pl.BlockSpec((lp, width), lambda b, p, s: (b, qb + 2 * n_pairs + p)),
            pl.BlockSpec((1,) + bias_tbl.shape[1:], lambda b, p, s: (p, 0, 0, 0)),
            pl.BlockSpec((1, 2, N_META), lambda b, p, s: (p, 0, 0)),
        ],
        out_specs=pl.BlockSpec((bps * GRID_W, width), lambda b, p, s: (b * nsteps + s, p)),
        out_shape=jax.ShapeDtypeStruct((t, n_pairs * width), bf16),
        compiler_params=_params("parallel", "parallel", "arbitrary"),
        name="neighborhood_attention",
    )(proj, proj, proj, bias_tbl, meta_bias.reshape(n_pairs, 2, N_META))


def _start_row_gather(src_hbm, idx_smem, slot, dst, sem, n):
    def body(i, c):
        pltpu.make_async_copy(src_hbm.at[pl.ds(idx_smem[slot, i], 1), :], dst.at[slot, pl.ds(i, 1), :], sem.at[slot]).start()
        return c

    lax.fori_loop(0, n, body, 0, unroll=8)


def _wait_row_gather(src_hbm, slot, dst, sem, n):
    pltpu.make_async_copy(src_hbm.at[pl.ds(0, n), :], dst.at[slot], sem.at[slot]).wait()


def _gather_pipeline(idx_hbm, src_hbm, idx_smem, buf, sem_idx, sem_rows, n):
    g = pl.program_id(0)
    last = pl.num_programs(0) - 1
    slot = g % 2
    nxt = 1 - slot

    def idx_copy(step, s):
        return pltpu.make_async_copy(idx_hbm.at[step], idx_smem.at[s], sem_idx.at[s])

    @pl.when(g == 0)
    def _():
        c = idx_copy(0, 0)
        c.start()
        c.wait()
        _start_row_gather(src_hbm, idx_smem, 0, buf, sem_rows, n)

        @pl.when(last >= 1)
        def _():
            idx_copy(1, 1).start()

    @pl.when(g < last)
    def _():
        idx_copy(g + 1, nxt).wait()
        _start_row_gather(src_hbm, idx_smem, nxt, buf, sem_rows, n)

    @pl.when(g + 1 < last)
    def _():
        idx_copy(g + 2, slot).start()

    _wait_row_gather(src_hbm, slot, buf, sem_rows, n)
    return slot


def _expert_kernel(ge_ref, nu_ref, idx_hbm, x_hbm, w1_ref, b1_ref, w2_ref, b2_ref, rw_ref, y_ref,
                   idx_smem, xbuf, sem_idx, sem_rows):
    del ge_ref
    slot = _gather_pipeline(idx_hbm, x_hbm, idx_smem, xbuf, sem_idx, sem_rows, GROUP_ROWS)
    g = pl.program_id(0)
    ff = w2_ref.shape[1]

    @pl.when(g < nu_ref[0])
    def _():
        x = xbuf[slot].astype(bf16)
        h = jnp.dot(x, w1_ref[0], preferred_element_type=f32) + b1_ref[0]
        x_glu = jnp.minimum(h[:, :ff], SWIGLU_LIMIT)
        x_lin = jnp.clip(h[:, ff:], -SWIGLU_LIMIT, SWIGLU_LIMIT)
        act = x_glu * jax.nn.sigmoid(SWIGLU_ALPHA * x_glu) * (x_lin + 1.0)
        y = jnp.dot(act.astype(bf16), w2_ref[0], preferred_element_type=f32) + b2_ref[0]
        y_ref[...] = y * rw_ref[...]

    @pl.when(g >= nu_ref[0])
    def _():
        y_ref[...] = jnp.zeros_like(y_ref)


def _expert_groups(h, row_tok, row_w, group_e, n_used, w1, b1, w2, b2):
    t, d = h.shape
    n_groups = row_tok.shape[0]
    n_exp, _, f2 = w1.shape
    ff = f2 // 2
    grid_spec = pltpu.PrefetchScalarGridSpec(
        num_scalar_prefetch=2,
        grid=(n_groups,),
        in_specs=[
            pl.BlockSpec(memory_space=pl.ANY),
            pl.BlockSpec(memory_space=pl.ANY),
            pl.BlockSpec((1, d, f2), lambda g, ge, nu: (ge[g], 0, 0)),
            pl.BlockSpec((1, 1, f2), lambda g, ge, nu: (ge[g], 0, 0)),
            pl.BlockSpec((1, ff, d), lambda g, ge, nu: (ge[g], 0, 0)),
            pl.BlockSpec((1, 1, d), lambda g, ge, nu: (ge[g], 0, 0)),
            pl.BlockSpec((GROUP_ROWS, 1), lambda g, ge, nu: (g, 0)),
        ],
        out_specs=pl.BlockSpec((GROUP_ROWS, d), lambda g, ge, nu: (g, 0)),
        scratch_shapes=[
            pltpu.SMEM((2, GROUP_ROWS), jnp.int32),
            pltpu.VMEM((2, GROUP_ROWS, d), f32),
            pltpu.SemaphoreType.DMA((2,)),
            pltpu.SemaphoreType.DMA((2,)),
        ],
    )
    return pl.pallas_call(
        _expert_kernel, grid_spec=grid_spec,
        out_shape=jax.ShapeDtypeStruct((n_groups * GROUP_ROWS, d), f32),
        compiler_params=_params("arbitrary"), name="expert_groups",
    )(group_e, n_used, row_tok, h, w1, b1.reshape(n_exp, 1, f2), w2, b2.reshape(n_exp, 1, d),
      row_w.reshape(n_groups * GROUP_ROWS, 1))


COMBINE_TOKENS = 128


def _combine_kernel(idx_hbm, y_hbm, o_ref, idx_smem, buf, sem_idx, sem_rows):
    n = TOP_K * COMBINE_TOKENS
    slot = _gather_pipeline(idx_hbm, y_hbm, idx_smem, buf, sem_idx, sem_rows, n)
    acc = buf[slot, 0:COMBINE_TOKENS, :]
    for kk in range(1, TOP_K):
        acc = acc + buf[slot, kk * COMBINE_TOKENS:(kk + 1) * COMBINE_TOKENS, :]
    o_ref[...] = acc


def _combine(y_rows, dest):
    t = dest.shape[0]
    d = y_rows.shape[1]
    tc = COMBINE_TOKENS
    n_tiles = t // tc
    idx = dest.reshape(n_tiles, tc, TOP_K).transpose(0, 2, 1).reshape(n_tiles, TOP_K * tc)
    return pl.pallas_call(
        _combine_kernel,
        grid=(n_tiles,),
        in_specs=[pl.BlockSpec(memory_space=pl.ANY), pl.BlockSpec(memory_space=pl.ANY)],
        out_specs=pl.BlockSpec((tc, d), lambda i: (i, 0)),
        out_shape=jax.ShapeDtypeStruct((t, d), f32),
        scratch_shapes=[
            pltpu.SMEM((2, TOP_K * tc), jnp.int32),
            pltpu.VMEM((2, TOP_K * tc, d), f32),
            pltpu.SemaphoreType.DMA((2,)),
            pltpu.SemaphoreType.DMA((2,)),
        ],
        compiler_params=_params("arbitrary"), name="moe_combine",
    )(idx, y_rows)


def _route(logits, n_exp):
    t = logits.shape[0]
    n = t * TOP_K
    top_v, top_e = lax.top_k(logits, TOP_K)
    gate = jax.nn.softmax(top_v, axis=-1)
    flat_e = top_e.reshape(-1).astype(jnp.int32)
    pos = jnp.arange(n, dtype=jnp.int32)
    e_sorted, order = lax.sort((flat_e, pos), num_keys=1, is_stable=True)
    _, inv = lax.sort((order, pos), num_keys=1)
    counts = jnp.sum(flat_e[:, None] == jnp.arange(n_exp, dtype=jnp.int32)[None, :], axis=0, dtype=jnp.int32)
    padded = (counts + GROUP_ROWS - 1) // GROUP_ROWS * GROUP_ROWS
    start = jnp.cumsum(counts) - counts
    pend = jnp.cumsum(padded)
    pstart = pend - padded
    dest_sorted = pstart[e_sorted] + (pos - start[e_sorted])
    dest = dest_sorted[inv].reshape(t, TOP_K)
    n_groups = -(-n // GROUP_ROWS) + n_exp
    group_e = jnp.clip(jnp.searchsorted(pend, jnp.arange(n_groups, dtype=jnp.int32) * GROUP_ROWS, side="right"),
                       0, n_exp - 1).astype(jnp.int32)
    rows = jnp.arange(n_groups * GROUP_ROWS, dtype=jnp.int32)
    row_e = jnp.repeat(group_e, GROUP_ROWS)
    within = rows - pstart[row_e]
    live = (within < counts[row_e]) & (rows < pend[n_exp - 1])
    src = jnp.clip(start[row_e] + within, 0, n - 1)
    row_tok = jnp.where(live, order[src] // TOP_K, 0).astype(jnp.int32)
    row_w = jnp.where(live, gate.reshape(-1)[order[src]], 0.0)
    n_used = (pend[n_exp - 1] // GROUP_ROWS).astype(jnp.int32).reshape(1)
    return (row_tok.reshape(n_groups, GROUP_ROWS), row_w.reshape(n_groups, GROUP_ROWS), group_e, n_used, dest)


def kernel(x, meta_tokens, ln0_w, ln0_b, w_in, gk_fwd_w, gk_fwd_b, gk_bwd_w, gk_bwd_b, gla_norm_w, w_branch_a, rpb, meta_bias, w_branch_b, w_out, ln1_w, ln1_b, router_w, router_b, w1, b1, w2, b2, ln2_w, ln2_b):
    bsz, seq, d = x.shape
    depth = w_in.shape[0]
    alpha = (2 * depth) ** 0.25
    lp = FRONT + seq
    kdim = gk_fwd_w.shape[2]
    vdim = w_branch_a.shape[1]
    ndim = w_branch_b.shape[1]
    n_exp = router_w.shape[2]
    ff = w2.shape[2]
    assert seq % GRID_W == 0 and n_exp <= SMALL_N and 2 * GATE_RANK <= SMALL_N

    lr_col = 2 * kdim + 2 * vdim
    nq_col = lr_col
    ga_col = nq_col + 3 * ndim
    gb_col = ga_col + d
    w_main = jnp.concatenate([w_in[:, :, :lr_col], w_in[:, :, lr_col + 2 * GATE_RANK:]], axis=2).astype(bf16)
    w_lr = jnp.pad(w_in[:, :, lr_col:lr_col + 2 * GATE_RANK], ((0, 0), (0, 0), (0, SMALL_N - 2 * GATE_RANK)))
    w_rt = jnp.pad(router_w, ((0, 0), (0, 0), (0, SMALL_N - n_exp)))
    b_rt = jnp.pad(router_b, ((0, 0), (0, SMALL_N - n_exp)))
    zeros_small = jnp.zeros((SMALL_N,), f32)
    w_a = w_branch_a.astype(bf16)
    w_b = w_branch_b.astype(bf16)
    w_o = w_out.astype(bf16)
    w1p = jnp.concatenate([w1[..., 0::2], w1[..., 1::2]], axis=-1).astype(bf16)
    b1p = jnp.concatenate([b1[..., 0::2], b1[..., 1::2]], axis=-1)
    w2b = w2.astype(bf16)
    bias_tbl = _na_bias_tables(rpb, seq // GRID_W)

    front = jnp.concatenate([jnp.zeros((PAD, d), x.dtype), meta_tokens.astype(x.dtype)], axis=0)
    xp = jnp.concatenate([jnp.broadcast_to(front[None], (bsz, FRONT, d)), x], axis=1).reshape(bsz * lp, d)
    h, hb, small = _layer_norm(xp, None, 1.0, ln0_w, ln0_b, w_lr[0], zeros_small)

    for l in range(depth):
        proj = _matmul(hb, w_main[l], bf16, "in_proj")
        lr_f = small[:, :GATE_RANK]
        lr_b = small[:, GATE_RANK:2 * GATE_RANK]
        o_bwd = _gla_direction(proj, lr_b, gk_bwd_w[l], gk_bwd_b[l], bsz, lp, reverse=True)
        o_a = _gla_direction(proj, lr_f, gk_fwd_w[l], gk_fwd_b[l], bsz, lp, reverse=False,
                             final_args=(o_bwd, gla_norm_w[l]))
        o_b = _neighborhood_attention(proj, bias_tbl[l], meta_bias[l], bsz, lp, nq_col)
        mixed = _branch_merge(o_a, o_b, w_a[l], w_b[l], proj, ga_col, gb_col)
        mix = _matmul(mixed, w_o[l], f32, "out_proj")
        h, hb, small = _layer_norm(mix, h, alpha, ln1_w[l], ln1_b[l], w_rt[l], b_rt[l])

        row_tok, row_w, group_e, n_used, dest = _route(small[:, :n_exp], n_exp)
        y_rows = _expert_groups(h, row_tok, row_w, group_e, n_used, w1p[l], b1p[l], w2b[l], b2[l])
        ffn = _combine(y_rows, dest)
        if l + 1 < depth:
            h, hb, small = _layer_norm(ffn, h, alpha, ln2_w[l], ln2_b[l], w_lr[l + 1], zeros_small)
        else:
            h, hb = _layer_norm(ffn, h, alpha, ln2_w[l], ln2_b[l])
    del ff, vdim
    return h.reshape(bsz, lp, d)[:, FRONT:]
```

```python
import functools

import numpy as np
import jax
import jax.numpy as jnp
from jax import lax
from jax.experimental import pallas as pl
from jax.experimental.pallas import tpu as pltpu

N_META = 16
GRID_W = 64
GLA_HEADS = 4
GATE_RANK = 16
GATE_NORMALIZER = 16.0
CHUNK = 64
NA_HEAD_DIM = 64
WIN_H = 8
WIN_W = 16
TOP_K = 4
SWIGLU_ALPHA = 1.702
SWIGLU_LIMIT = 7.0
GROUP_ROWS = 256
LN_EPS = 1e-5
RMS_EPS = 1e-6

FRONT = CHUNK
PAD = FRONT - N_META
SMALL_N = 128
NEG = -1e30
VMEM_LIMIT = 52 * 1024 * 1024
HIGHEST = lax.Precision.HIGHEST
NT = (((1,), (1,)), ((), ()))
TN = (((0,), (0,)), ((), ()))

f32 = jnp.float32
bf16 = jnp.bfloat16


def _params(*sem):
    return pltpu.CompilerParams(dimension_semantics=sem, vmem_limit_bytes=VMEM_LIMIT)


def _tile(n, prefs):
    for p in prefs:
        if n % p == 0:
            return p
    raise ValueError(f"no tile in {prefs} divides {n}")


def _ln_kernel(*refs, alpha, has_mm, has_res, has_proj):
    it = iter(refs)
    x_ref = next(it)
    mw_ref = next(it) if has_mm else None
    res_ref = next(it) if has_res else None
    w_ref, b_ref = next(it), next(it)
    pw_ref, pb_ref = (next(it), next(it)) if has_proj else (None, None)
    h_ref, hb_ref = next(it), next(it)
    if has_mm:
        z = jnp.dot(x_ref[...], mw_ref[...], preferred_element_type=f32)
    else:
        z = x_ref[...]
    if has_res:
        z = alpha * res_ref[...] + z
    mu = jnp.mean(z, axis=-1, keepdims=True)
    d = z - mu
    var = jnp.mean(d * d, axis=-1, keepdims=True)
    h = d * lax.rsqrt(var + LN_EPS) * w_ref[...] + b_ref[...]
    h_ref[...] = h
    hb_ref[...] = h.astype(bf16)
    if has_proj:
        p_ref = next(it)
        h_hi = h.astype(bf16)
        h_lo = (h - h_hi.astype(f32)).astype(bf16)
        lhs = jnp.concatenate([h_hi, h_hi, h_lo], axis=1)
        p_ref[...] = jnp.dot(lhs, pw_ref[...], preferred_element_type=f32) + pb_ref[...]


def _layer_norm(x, res, alpha, w, b, pw=None, pb=None, mm_w=None):
    t = x.shape[0]
    d = w.shape[0]
    tm = _tile(t, (256, 128, 64))
    has_mm, has_res, has_proj = mm_w is not None, res is not None, pw is not None
    row = pl.BlockSpec((tm, d), lambda i: (i, 0))
    vec = pl.BlockSpec((1, d), lambda i: (0, 0))
    ins, specs = [x], [pl.BlockSpec((tm, x.shape[1]), lambda i: (i, 0))]
    if has_mm:
        ins.append(mm_w)
        specs.append(pl.BlockSpec(mm_w.shape, lambda i: (0, 0)))
    if has_res:
        ins.append(res)
        specs.append(row)
    ins += [w.reshape(1, d), b.reshape(1, d)]
    specs += [vec, vec]
    outs = [jax.ShapeDtypeStruct((t, d), f32), jax.ShapeDtypeStruct((t, d), bf16)]
    out_specs = [row, row]
    if has_proj:
        ins += [pw, pb.reshape(1, SMALL_N)]
        specs += [pl.BlockSpec((3 * d, SMALL_N), lambda i: (0, 0)), pl.BlockSpec((1, SMALL_N), lambda i: (0, 0))]
        outs.append(jax.ShapeDtypeStruct((t, SMALL_N), f32))
        out_specs.append(pl.BlockSpec((tm, SMALL_N), lambda i: (i, 0)))
    return pl.pallas_call(
        functools.partial(_ln_kernel, alpha=alpha, has_mm=has_mm, has_res=has_res, has_proj=has_proj),
        grid=(t // tm,), in_specs=specs, out_specs=out_specs, out_shape=outs,
        compiler_params=_params("parallel"), name="layer_norm",
    )(*ins)


def _mm_kernel(x_ref, w_ref, o_ref):
    o_ref[...] = jnp.dot(x_ref[...], w_ref[...], preferred_element_type=f32).astype(o_ref.dtype)


def _matmul(x, w, out_dtype, name):
    m, k = x.shape
    n = w.shape[1]
    tm = _tile(m, (768, 512, 384, 256, 128))
    tn = _tile(n, (1024, 512, 256, 128))
    return pl.pallas_call(
        _mm_kernel,
        grid=(n // tn, m // tm),
        in_specs=[pl.BlockSpec((tm, k), lambda j, i: (i, 0)), pl.BlockSpec((k, tn), lambda j, i: (0, j))],
        out_specs=pl.BlockSpec((tm, tn), lambda j, i: (i, j)),
        out_shape=jax.ShapeDtypeStruct((m, n), out_dtype),
        compiler_params=_params("parallel", "parallel"), name=name,
    )(x, w)


LANE_PAIR = 256


def _deinterleave_kernel(x_ref, p_ref, o_ref):
    half = o_ref.shape[-1] // 2
    out_w = LANE_PAIR // 2
    for j in range(x_ref.shape[-1] // LANE_PAIR):
        xb = x_ref[0, :, j * LANE_PAIR:(j + 1) * LANE_PAIR].astype(bf16)
        y = jnp.dot(xb, p_ref[...], preferred_element_type=f32).astype(bf16)
        o_ref[0, :, j * out_w:(j + 1) * out_w] = y[:, :out_w]
        o_ref[0, :, half + j * out_w:half + (j + 1) * out_w] = y[:, out_w:]


def _deinterleave_cast(w):
    n, d, f2 = w.shape
    assert f2 % LANE_PAIR == 0
    tr = _tile(d, (1024, 512, 256, 128))
    perm = np.zeros((LANE_PAIR, LANE_PAIR), np.float32)
    idx = np.arange(LANE_PAIR // 2)
    perm[2 * idx, idx] = 1.0
    perm[2 * idx + 1, LANE_PAIR // 2 + idx] = 1.0
    return pl.pallas_call(
        _deinterleave_kernel,
        grid=(n, d // tr),
        in_specs=[pl.BlockSpec((1, tr, f2), lambda i, r: (i, r, 0)),
                  pl.BlockSpec((LANE_PAIR, LANE_PAIR), lambda i, r: (0, 0))],
        out_specs=pl.BlockSpec((1, tr, f2), lambda i, r: (i, r, 0)),
        out_shape=jax.ShapeDtypeStruct((n, d, f2), bf16),
        compiler_params=_params("parallel", "parallel"), name="deinterleave_cast",
    )(w, jnp.asarray(perm, bf16))


def _split_lhs(a):
    hi = a.astype(bf16)
    lo = (a - hi.astype(f32)).astype(bf16)
    return jnp.concatenate([hi, hi, lo], axis=1)


def _split_rhs(w):
    hi = w.astype(bf16)
    lo = (w - hi.astype(f32)).astype(bf16)
    return jnp.concatenate([hi, lo, hi], axis=-2)


def _branch_kernel(oa_ref, ob_ref, wa_ref, wb_ref, ga_ref, gb_ref, o_ref):
    ya = jnp.dot(oa_ref[...], wa_ref[...], preferred_element_type=f32)
    yb = jnp.dot(ob_ref[...], wb_ref[...], preferred_element_type=f32)
    mixed = jax.nn.sigmoid(ga_ref[...].astype(f32)) * ya + jax.nn.sigmoid(gb_ref[...].astype(f32)) * yb
    o_ref[...] = mixed.astype(o_ref.dtype)


def _branch_merge(o_a, o_b, w_a, w_b, proj, ga_col, gb_col):
    t, ka = o_a.shape
    kb = o_b.shape[1]
    d = w_a.shape[1]
    tm = _tile(t, (768, 512, 384, 256, 128))
    tn = 512
    assert ga_col % tn == 0 and gb_col % tn == 0 and d % tn == 0
    return pl.pallas_call(
        _branch_kernel,
        grid=(d // tn, t // tm),
        in_specs=[
            pl.BlockSpec((tm, ka), lambda j, i: (i, 0)),
            pl.BlockSpec((tm, kb), lambda j, i: (i, 0)),
            pl.BlockSpec((ka, tn), lambda j, i: (0, j)),
            pl.BlockSpec((kb, tn), lambda j, i: (0, j)),
            pl.BlockSpec((tm, tn), lambda j, i: (i, ga_col // tn + j)),
            pl.BlockSpec((tm, tn), lambda j, i: (i, gb_col // tn + j)),
        ],
        out_specs=pl.BlockSpec((tm, tn), lambda j, i: (i, j)),
        out_shape=jax.ShapeDtypeStruct((t, d), bf16),
        compiler_params=_params("parallel", "parallel"), name="branch_merge",
    )(o_a, o_b, w_a, w_b, proj, proj)


N_LEVELS = 6
DIAG_LEVEL = N_LEVELS + 1
KDEC_BLOCK = N_LEVELS + 1


def _gla_constants(reverse):
    c = CHUNK
    i = np.arange(c)[:, None]
    s = np.arange(c)[None, :]
    blocks = [(s >= i) if reverse else (s <= i)]
    level = np.zeros((c, c), np.int32)
    size = c
    for lv in range(1, N_LEVELS + 1):
        half = size // 2
        m = np.zeros((c, c), bool)
        for r in range(c):
            base = (r // size) * size
            first_half = (r % size) < half
            if not reverse:
                mid = base + half - 1
                if first_half:
                    m[r, r + 1:mid + 1] = True
                else:
                    m[r, mid + 1:r + 1] = True
            else:
                mid = base + half
                if first_half:
                    m[r, r:mid] = True
                else:
                    m[r, mid:r] = True
        blocks.append(m)
        same = (i // size) == (s // size)
        if not reverse:
            cond = same & ((i % size) >= half) & ((s % size) < half)
        else:
            cond = same & ((i % size) < half) & ((s % size) >= half)
        level[cond] = lv
        size = half
    level[np.eye(c, dtype=bool)] = DIAG_LEVEL
    blocks.append((s < i) if reverse else (s > i))
    return np.concatenate(blocks, axis=0).astype(np.float32), level


def _gla_kernel(*refs, reverse, cps, final, dk):
    it = iter(refs)
    q_ref, k_ref, v_ref, lr_ref, gw_ref, gb_ref, ms_ref, lvl_ref = (next(it) for _ in range(8))
    if final:
        r_ref, ob_ref, nw_ref = next(it), next(it), next(it)
    o_ref = next(it)
    st_ref = next(it)
    step = pl.program_id(2)
    nsteps = pl.num_programs(2)

    @pl.when(step == 0)
    def _():
        st_ref[...] = jnp.zeros_like(st_ref)

    blk = (nsteps - 1 - step) if reverse else step
    ms = ms_ref[...]
    lvl = lvl_ref[...]
    row = lax.broadcasted_iota(jnp.int32, (CHUNK, 1), 0)
    scale = dk ** -0.5
    order = list(range(cps))[::-1] if reverse else list(range(cps))
    rows = [slice(c * CHUNK, (c + 1) * CHUNK) for c in range(cps)]

    q, k, v, e = {}, {}, {}, {}
    for c in order:
        valid = (row >= PAD) | (blk * cps + c > 0)
        q[c] = jnp.where(valid, q_ref[rows[c], :].astype(f32) * scale, 0.0)
        k[c] = jnp.where(valid, k_ref[rows[c], :].astype(f32), 0.0)
        v[c] = jnp.where(valid, v_ref[rows[c], :], jnp.zeros((), bf16))
        x = jnp.dot(lr_ref[rows[c], :], gw_ref[...], preferred_element_type=f32) + gb_ref[...]
        g = (jnp.minimum(x, 0.0) - jnp.log1p(jnp.exp(-jnp.abs(x)))) * (1.0 / GATE_NORMALIZER)
        g = jnp.where(valid, g, 0.0)
        g_hi = g.astype(bf16)
        g_lo = (g - g_hi.astype(f32)).astype(bf16)
        e[c] = jnp.dot(ms, jnp.concatenate([g_hi, g_lo], axis=0), preferred_element_type=f32)
    scores = {c: jnp.zeros((CHUNK, CHUNK), f32) for c in order}
    for lv in range(1, N_LEVELS + 1):
        for c in order:
            w = jnp.exp(e[c][lv * CHUNK:(lv + 1) * CHUNK])
            a = lax.dot_general((q[c] * w).astype(bf16), (k[c] * w).astype(bf16), NT, preferred_element_type=f32)
            scores[c] = jnp.where(lvl == lv, a, scores[c])
    qi, upd, total, o_intra = {}, {}, {}, {}
    for c in order:
        a = lax.dot_general(q[c].astype(bf16), k[c].astype(bf16), NT, preferred_element_type=f32)
        scores[c] = jnp.where(lvl == DIAG_LEVEL, a, scores[c])
        cum = e[c][0:CHUNK]
        qi[c] = (q[c] * jnp.exp(cum)).astype(bf16)
        kd = (k[c] * jnp.exp(e[c][KDEC_BLOCK * CHUNK:(KDEC_BLOCK + 1) * CHUNK])).astype(bf16)
        upd[c] = lax.dot_general(v[c], kd, TN, preferred_element_type=f32)
        total[c] = jnp.exp(cum[0:1] if reverse else cum[CHUNK - 1:CHUNK])
    for c in order:
        o_intra[c] = jnp.dot(scores[c].astype(bf16), v[c], preferred_element_type=f32)

    st = st_ref[...]
    for c in order:
        o = lax.dot_general(qi[c], st.astype(bf16), NT, preferred_element_type=f32) + o_intra[c]
        st = st * total[c] + upd[c]
        if final:
            o = o + ob_ref[rows[c], :]
            o = o * lax.rsqrt(jnp.mean(o * o, axis=-1, keepdims=True) + RMS_EPS) * nw_ref[...]
            o = o * jax.nn.silu(r_ref[rows[c], :].astype(f32))
        o_ref[rows[c], :] = o.astype(o_ref.dtype)
    st_ref[...] = st


def _gla_direction(proj, lr, gw, gb, bsz, lp, reverse, final_args=None):
    t = proj.shape[0]
    heads = GLA_HEADS
    dk = gw.shape[1] // heads
    dv = 2 * dk
    n_chunks = lp // CHUNK
    cps = _tile(n_chunks, (3, 2, 1))
    rb = cps * CHUNK
    nsteps = n_chunks // cps
    final = final_args is not None
    ms, lvl = _gla_constants(reverse)
    ms = np.concatenate([ms, ms], axis=1)
    nblk = ms.shape[0]
    rank3 = lr.shape[1]

    def rblk(b, s):
        return b * nsteps + ((nsteps - 1 - s) if reverse else s)

    k_off = heads
    v_off = heads
    r_off = 2 * heads
    in_specs = [
        pl.BlockSpec((rb, dk), lambda b, h, s: (rblk(b, s), h)),
        pl.BlockSpec((rb, dk), lambda b, h, s: (rblk(b, s), k_off + h)),
        pl.BlockSpec((rb, dv), lambda b, h, s: (rblk(b, s), v_off + h)),
        pl.BlockSpec((rb, rank3), lambda b, h, s: (rblk(b, s), 0)),
        pl.BlockSpec((rank3, dk), lambda b, h, s: (0, h)),
        pl.BlockSpec((1, dk), lambda b, h, s: (0, h)),
        pl.BlockSpec((nblk, 2 * CHUNK), lambda b, h, s: (0, 0)),
        pl.BlockSpec((CHUNK, CHUNK), lambda b, h, s: (0, 0)),
    ]
    ins = [proj, proj, proj, lr, gw, gb.reshape(1, -1), jnp.asarray(ms, bf16), jnp.asarray(lvl)]
    if final:
        o_other, nw = final_args
        in_specs += [
            pl.BlockSpec((rb, dv), lambda b, h, s: (rblk(b, s), r_off + h)),
            pl.BlockSpec((rb, dv), lambda b, h, s: (rblk(b, s), h)),
            pl.BlockSpec((1, dv), lambda b, h, s: (0, 0)),
        ]
        ins += [proj, o_other, nw.reshape(1, dv)]
    return pl.pallas_call(
        functools.partial(_gla_kernel, reverse=reverse, cps=cps, final=final, dk=dk),
        grid=(bsz, heads, nsteps),
        in_specs=in_specs,
        out_specs=pl.BlockSpec((rb, dv), lambda b, h, s: (rblk(b, s), h)),
        out_shape=jax.ShapeDtypeStruct((t, heads * dv), bf16 if final else f32),
        scratch_shapes=[pltpu.VMEM((dv, dk), f32)],
        compiler_params=_params("parallel", "parallel", "arbitrary"),
        name="gla_final" if final else "gla_partial",
    )(*ins)


N_VARIANTS = WIN_H


def _na_bias_tables(rpb, grid_rows):
    del grid_rows
    n_layers, heads = rpb.shape[0], rpb.shape[1]
    cols = np.arange(GRID_W)
    col_start = np.clip(cols - WIN_W // 2, 0, GRID_W - WIN_W)
    kc = np.arange(GRID_W)
    in_win = (kc[None, :] >= col_start[:, None]) & (kc[None, :] < col_start[:, None] + WIN_W)
    dc = np.clip(kc[None, :] - cols[:, None] + (WIN_W - 1), 0, 2 * WIN_W - 2)
    rpb_c = jnp.where(in_win[None, None, None], rpb[:, :, :, dc], NEG)
    tbl = jnp.stack([rpb_c[:, :, vi:vi + WIN_H] for vi in range(N_VARIANTS)], axis=2)
    tbl = tbl.reshape(n_layers, heads // 2, 2, N_VARIANTS, WIN_H, GRID_W, GRID_W)
    tbl = jnp.transpose(tbl, (0, 1, 3, 2, 5, 4, 6))
    return tbl.reshape(n_layers, heads // 2, N_VARIANTS, 2 * GRID_W, WIN_H * GRID_W)


def _na_kernel(q_ref, k_ref, v_ref, bias_ref, mb_ref, o_ref, *, bps, grid_rows):
    step = pl.program_id(2)
    lane = lax.broadcasted_iota(jnp.int32, (1, 2 * NA_HEAD_DIM), 1)
    head0 = lane < NA_HEAD_DIM
    km = k_ref[PAD:FRONT, :]
    vm = v_ref[PAD:FRONT, :]
    mb = mb_ref[0]
    mbias = jnp.concatenate([jnp.broadcast_to(mb[0:1], (GRID_W, N_META)),
                             jnp.broadcast_to(mb[1:2], (GRID_W, N_META))], axis=0)
    zero = jnp.zeros((), bf16)
    scale = NA_HEAD_DIM ** -0.5

    def stacked_queries(j):
        q2 = q_ref[j * GRID_W:(j + 1) * GRID_W, :] * jnp.asarray(scale, bf16)
        return jnp.concatenate([jnp.where(head0, q2, zero), jnp.where(head0, zero, q2)], axis=0)

    def unstack(o):
        return jnp.where(head0, o[0:GRID_W], o[GRID_W:2 * GRID_W])

    row = lax.broadcasted_iota(jnp.int32, (GRID_W, 1), 0)
    blocks = range(bps)
    qq, vv, sl, sm, front = {}, {}, {}, {}, {}
    for j in blocks:
        g = step * bps + j
        r = g - 1
        r0 = jnp.clip(r - WIN_H // 2, 0, grid_rows - WIN_H)
        vi = jnp.clip(r0 - r + WIN_H // 2 + 3, 0, N_VARIANTS - 1)
        ks = pl.multiple_of((r0 + 1) * GRID_W, GRID_W)
        kk = k_ref[pl.ds(ks, WIN_H * GRID_W), :]
        vv[j] = v_ref[pl.ds(ks, WIN_H * GRID_W), :]
        qq[j] = stacked_queries(j)
        front[j] = g == 0
        sl[j] = lax.dot_general(qq[j], kk, NT, preferred_element_type=f32) + bias_ref[0, vi]
        sm[j] = lax.dot_general(qq[j], km, NT, preferred_element_type=f32) + mbias
    p_loc, p_meta, den = {}, {}, {}
    for j in blocks:
        s_loc = sl[j] + jnp.where(front[j], NEG, 0.0) if j == 0 else sl[j]
        mx = jnp.maximum(jnp.max(s_loc, axis=-1, keepdims=True), jnp.max(sm[j], axis=-1, keepdims=True))
        p_loc[j] = jnp.exp(s_loc - mx)
        p_meta[j] = jnp.exp(sm[j] - mx)
        den[j] = jnp.sum(p_loc[j], axis=-1, keepdims=True) + jnp.sum(p_meta[j], axis=-1, keepdims=True)
    for j in blocks:
        o = (jnp.dot(p_loc[j].astype(bf16), vv[j], preferred_element_type=f32)
             + jnp.dot(p_meta[j].astype(bf16), vm, preferred_element_type=f32))
        o = unstack(o / den[j])
        if j == 0:
            o = jnp.where((row >= PAD) | jnp.logical_not(front[j]), o, 0.0)
        o_ref[j * GRID_W:(j + 1) * GRID_W, :] = o.astype(o_ref.dtype)


def _neighborhood_attention(proj, bias_tbl, meta_bias, bsz, lp, q_col):
    t = proj.shape[0]
    n_pairs = bias_tbl.shape[0]
    width = 2 * NA_HEAD_DIM
    grid_rows = (lp - FRONT) // GRID_W
    assert grid_rows >= WIN_H
    n_blocks = lp // GRID_W
    bps = _tile(n_blocks, (3, 2, 1))
    nsteps = n_blocks // bps
    qb = q_col // width
    return pl.pallas_call(
        functools.partial(_na_kernel, bps=bps, grid_rows=grid_rows),
        grid=(bsz, n_pairs, nsteps),
        in_specs=[
            pl.BlockSpec((bps * GRID_W, width), lambda b, p, s: (b * nsteps + s, qb + p)),
            pl.BlockSpec((lp, width), lambda b, p, s: (b, qb + n_pairs + p)),
            pl.BlockSpec((lp, width), lambda b, p, s: (b, qb + 2 * n_pairs + p)),
            pl.BlockSpec((1,) + bias_tbl.shape[1:], lambda b, p, s: (p, 0, 0, 0)),
            pl.BlockSpec((1, 2, N_META), lambda b, p, s: (p, 0, 0)),
        ],
        out_specs=pl.BlockSpec((bps * GRID_W, width), lambda b, p, s: (b * nsteps + s, p)),
        out_shape=jax.ShapeDtypeStruct((t, n_pairs * width), bf16),
        compiler_params=_params("parallel", "parallel", "arbitrary"),
        name="neighborhood_attention",
    )(proj, proj, proj, bias_tbl, meta_bias.reshape(n_pairs, 2, N_META))


SUBLANES = 8


def _start_row_gather(src_hbm, idx_smem, slot, dst, sem, n):
    base = slot * n

    def body(j, c):
        r0 = pl.multiple_of(j * SUBLANES, SUBLANES)
        for u in range(SUBLANES):
            tok = idx_smem[base + r0 + u]
            pltpu.make_async_copy(src_hbm.at[pl.ds(tok, 1), :], dst.at[slot, pl.ds(r0 + u, 1), :], sem.at[slot]).start()
        return c

    lax.fori_loop(0, n // SUBLANES, body, 0)


def _wait_row_gather(src_hbm, slot, dst, sem, n):
    pltpu.make_async_copy(src_hbm.at[pl.ds(0, n), :], dst.at[slot], sem.at[slot]).wait()


def _gather_pipeline(idx_hbm, src_hbm, idx_smem, buf, sem_idx, sem_rows, n):
    g = pl.program_id(0)
    last = pl.num_programs(0) - 1
    slot = g % 2
    nxt = 1 - slot

    def idx_copy(step, s):
        return pltpu.make_async_copy(idx_hbm.at[step], idx_smem.at[pl.ds(s * n, n)], sem_idx.at[s])

    @pl.when(g == 0)
    def _():
        c = idx_copy(0, 0)
        c.start()
        c.wait()
        _start_row_gather(src_hbm, idx_smem, 0, buf, sem_rows, n)

        @pl.when(last >= 1)
        def _():
            idx_copy(1, 1).start()

    @pl.when(g < last)
    def _():
        idx_copy(g + 1, nxt).wait()
        _start_row_gather(src_hbm, idx_smem, nxt, buf, sem_rows, n)

    @pl.when(g + 1 < last)
    def _():
        idx_copy(g + 2, slot).start()

    _wait_row_gather(src_hbm, slot, buf, sem_rows, n)
    return slot


def _expert_kernel(ge_ref, nu_ref, idx_hbm, x_hbm, w1_ref, b1_ref, w2_ref, b2_ref, rw_ref, y_ref,
                   idx_smem, xbuf, sem_idx, sem_rows):
    del ge_ref
    slot = _gather_pipeline(idx_hbm, x_hbm, idx_smem, xbuf, sem_idx, sem_rows, GROUP_ROWS)
    g = pl.program_id(0)
    ff = w2_ref.shape[1]

    @pl.when(g < nu_ref[0])
    def _():
        x = xbuf[slot].astype(bf16)
        h = jnp.dot(x, w1_ref[0], preferred_element_type=f32) + b1_ref[0]
        x_glu = jnp.minimum(h[:, :ff], SWIGLU_LIMIT)
        x_lin = jnp.clip(h[:, ff:], -SWIGLU_LIMIT, SWIGLU_LIMIT)
        act = x_glu * jax.nn.sigmoid(SWIGLU_ALPHA * x_glu) * (x_lin + 1.0)
        y = jnp.dot(act.astype(bf16), w2_ref[0], preferred_element_type=f32) + b2_ref[0]
        y_ref[...] = y * rw_ref[...]

    @pl.when(g >= nu_ref[0])
    def _():
        y_ref[...] = jnp.zeros_like(y_ref)


def _expert_groups(h, row_tok, row_w, group_e, n_used, w1, b1, w2, b2):
    t, d = h.shape
    n_groups = row_tok.shape[0]
    n_exp, _, f2 = w1.shape
    ff = f2 // 2
    grid_spec = pltpu.PrefetchScalarGridSpec(
        num_scalar_prefetch=2,
        grid=(n_groups,),
        in_specs=[
            pl.BlockSpec(memory_space=pl.ANY),
            pl.BlockSpec(memory_space=pl.ANY),
            pl.BlockSpec((1, d, f2), lambda g, ge, nu: (ge[g], 0, 0)),
            pl.BlockSpec((1, 1, f2), lambda g, ge, nu: (ge[g], 0, 0)),
            pl.BlockSpec((1, ff, d), lambda g, ge, nu: (ge[g], 0, 0)),
            pl.BlockSpec((1, 1, d), lambda g, ge, nu: (ge[g], 0, 0)),
            pl.BlockSpec((GROUP_ROWS, 1), lambda g, ge, nu: (g, 0)),
        ],
        out_specs=pl.BlockSpec((GROUP_ROWS, d), lambda g, ge, nu: (g, 0)),
        scratch_shapes=[
            pltpu.SMEM((2 * GROUP_ROWS,), jnp.int32),
            pltpu.VMEM((2, GROUP_ROWS, d), f32),
            pltpu.SemaphoreType.DMA((2,)),
            pltpu.SemaphoreType.DMA((2,)),
        ],
    )
    return pl.pallas_call(
        _expert_kernel, grid_spec=grid_spec,
        out_shape=jax.ShapeDtypeStruct((n_groups * GROUP_ROWS, d), f32),
        compiler_params=_params("arbitrary"), name="expert_groups",
    )(group_e, n_used, row_tok, h, w1, b1.reshape(n_exp, 1, f2), w2, b2.reshape(n_exp, 1, d),
      row_w.reshape(n_groups * GROUP_ROWS, 1))


COMBINE_TOKENS = 128


def _combine_kernel(idx_hbm, y_hbm, o_ref, idx_smem, buf, sem_idx, sem_rows):
    n = TOP_K * COMBINE_TOKENS
    slot = _gather_pipeline(idx_hbm, y_hbm, idx_smem, buf, sem_idx, sem_rows, n)
    acc = buf[slot, 0:COMBINE_TOKENS, :]
    for kk in range(1, TOP_K):
        acc = acc + buf[slot, kk * COMBINE_TOKENS:(kk + 1) * COMBINE_TOKENS, :]
    o_ref[...] = acc


def _combine(y_rows, dest):
    t = dest.shape[0]
    d = y_rows.shape[1]
    tc = COMBINE_TOKENS
    n_tiles = t // tc
    idx = dest.reshape(n_tiles, tc, TOP_K).transpose(0, 2, 1).reshape(n_tiles, TOP_K * tc)
    return pl.pallas_call(
        _combine_kernel,
        grid=(n_tiles,),
        in_specs=[pl.BlockSpec(memory_space=pl.ANY), pl.BlockSpec(memory_space=pl.ANY)],
        out_specs=pl.BlockSpec((tc, d), lambda i: (i, 0)),
        out_shape=jax.ShapeDtypeStruct((t, d), f32),
        scratch_shapes=[
            pltpu.SMEM((2 * TOP_K * tc,), jnp.int32),
            pltpu.VMEM((2, TOP_K * tc, d), f32),
            pltpu.SemaphoreType.DMA((2,)),
            pltpu.SemaphoreType.DMA((2,)),
        ],
        compiler_params=_params("arbitrary"), name="moe_combine",
    )(idx, y_rows)


def _route(logits, n_exp):
    t = logits.shape[0]
    n = t * TOP_K
    top_v, top_e = lax.top_k(logits, TOP_K)
    gate = jax.nn.softmax(top_v, axis=-1)
    flat_e = top_e.reshape(-1).astype(jnp.int32)
    pos = jnp.arange(n, dtype=jnp.int32)
    e_sorted, order = lax.sort((flat_e, pos), num_keys=1, is_stable=True)
    _, inv = lax.sort((order, pos), num_keys=1)
    bounds = jnp.searchsorted(e_sorted, jnp.arange(n_exp + 1, dtype=jnp.int32), side="left",
                              method="scan_unrolled").astype(jnp.int32)
    start = bounds[:n_exp]
    counts = bounds[1:] - start
    padded = (counts + GROUP_ROWS - 1) // GROUP_ROWS * GROUP_ROWS
    pend = jnp.cumsum(padded)
    pstart = pend - padded
    dest_sorted = pstart[e_sorted] + (pos - start[e_sorted])
    dest = dest_sorted[inv].reshape(t, TOP_K)
    n_groups = -(-n // GROUP_ROWS) + n_exp
    group_row0 = jnp.arange(n_groups, dtype=jnp.int32) * GROUP_ROWS
    group_e = jnp.minimum(jnp.sum(pend[None, :] <= group_row0[:, None], axis=1, dtype=jnp.int32), n_exp - 1)
    rows = jnp.arange(n_groups * GROUP_ROWS, dtype=jnp.int32)
    row_e = jnp.repeat(group_e, GROUP_ROWS)
    within = rows - pstart[row_e]
    live = (within < counts[row_e]) & (rows < pend[n_exp - 1])
    src = jnp.clip(start[row_e] + within, 0, n - 1)
    row_tok = jnp.where(live, order[src] // TOP_K, 0).astype(jnp.int32)
    row_w = jnp.where(live, gate.reshape(-1)[order[src]], 0.0)
    n_used = (pend[n_exp - 1] // GROUP_ROWS).astype(jnp.int32).reshape(1)
    return (row_tok.reshape(n_groups, GROUP_ROWS), row_w.reshape(n_groups, GROUP_ROWS), group_e, n_used, dest)


def kernel(x, meta_tokens, ln0_w, ln0_b, w_in, gk_fwd_w, gk_fwd_b, gk_bwd_w, gk_bwd_b, gla_norm_w, w_branch_a, rpb, meta_bias, w_branch_b, w_out, ln1_w, ln1_b, router_w, router_b, w1, b1, w2, b2, ln2_w, ln2_b):
    bsz, seq, d = x.shape
    depth = w_in.shape[0]
    alpha = (2 * depth) ** 0.25
    lp = FRONT + seq
    kdim = gk_fwd_w.shape[2]
    vdim = w_branch_a.shape[1]
    ndim = w_branch_b.shape[1]
    n_exp = router_w.shape[2]
    ff = w2.shape[2]
    assert seq % GRID_W == 0 and n_exp <= SMALL_N and 2 * GATE_RANK <= SMALL_N

    lr_col = 2 * kdim + 2 * vdim
    nq_col = lr_col
    ga_col = nq_col + 3 * ndim
    gb_col = ga_col + d
    w_main = jnp.concatenate([w_in[:, :, :lr_col], w_in[:, :, lr_col + 2 * GATE_RANK:]], axis=2).astype(bf16)
    w_lr = _split_rhs(jnp.pad(w_in[:, :, lr_col:lr_col + 2 * GATE_RANK], ((0, 0), (0, 0), (0, SMALL_N - 2 * GATE_RANK))))
    w_rt = _split_rhs(jnp.pad(router_w, ((0, 0), (0, 0), (0, SMALL_N - n_exp))))
    b_rt = jnp.pad(router_b, ((0, 0), (0, SMALL_N - n_exp)))
    zeros_small = jnp.zeros((SMALL_N,), f32)
    w_a = w_branch_a.astype(bf16)
    w_b = w_branch_b.astype(bf16)
    w_o = w_out.astype(bf16)
    w1p = _deinterleave_cast(w1.reshape(depth * n_exp, d, 2 * ff)).reshape(depth, n_exp, d, 2 * ff)
    b1p = jnp.concatenate([b1[..., 0::2], b1[..., 1::2]], axis=-1)
    w2b = w2.astype(bf16)
    bias_tbl = _na_bias_tables(rpb, seq // GRID_W)
    gw_f, gw_b = _split_rhs(gk_fwd_w), _split_rhs(gk_bwd_w)

    front = jnp.concatenate([jnp.zeros((PAD, d), x.dtype), meta_tokens.astype(x.dtype)], axis=0)
    xp = jnp.concatenate([jnp.broadcast_to(front[None], (bsz, FRONT, d)), x], axis=1).reshape(bsz * lp, d)
    h, hb, small = _layer_norm(xp, None, 1.0, ln0_w, ln0_b, w_lr[0], zeros_small)

    for l in range(depth):
        proj = _matmul(hb, w_main[l], bf16, "in_proj")
        lr_f = _split_lhs(small[:, :GATE_RANK])
        lr_b = _split_lhs(small[:, GATE_RANK:2 * GATE_RANK])
        o_bwd = _gla_direction(proj, lr_b, gw_b[l], gk_bwd_b[l], bsz, lp, reverse=True)
        o_a = _gla_direction(proj, lr_f, gw_f[l], gk_fwd_b[l], bsz, lp, reverse=False,
                             final_args=(o_bwd, gla_norm_w[l]))
        o_b = _neighborhood_attention(proj, bias_tbl[l], meta_bias[l], bsz, lp, nq_col)
        mixed = _branch_merge(o_a, o_b, w_a[l], w_b[l], proj, ga_col, gb_col)
        h, hb, small = _layer_norm(mixed, h, alpha, ln1_w[l], ln1_b[l], w_rt[l], b_rt[l], mm_w=w_o[l])

        row_tok, row_w, group_e, n_used, dest = _route(small[:, :n_exp], n_exp)
        y_rows = _expert_groups(h, row_tok, row_w, group_e, n_used, w1p[l], b1p[l], w2b[l], b2[l])
        ffn = _combine(y_rows, dest)
        if l + 1 < depth:
            h, hb, small = _layer_norm(ffn, h, alpha, ln2_w[l], ln2_b[l], w_lr[l + 1], zeros_small)
        else:
            h, hb = _layer_norm(ffn, h, alpha, ln2_w[l], ln2_b[l])
    del ff, vdim
    return h.reshape(bsz, lp, d)[:, FRONT:]
```

```python
import functools

import numpy as np
import jax
import jax.numpy as jnp
from jax import lax
from jax.experimental import pallas as pl
from jax.experimental.pallas import tpu as pltpu

N_META = 16
GRID_W = 64
GLA_HEADS = 4
GATE_RANK = 16
GATE_NORMALIZER = 16.0
CHUNK = 64
NA_HEAD_DIM = 64
WIN_H = 8
WIN_W = 16
TOP_K = 4
SWIGLU_ALPHA = 1.702
SWIGLU_LIMIT = 7.0
GROUP_ROWS = 256
LN_EPS = 1e-5
RMS_EPS = 1e-6

FRONT = CHUNK
PAD = FRONT - N_META
SMALL_N = 128
NEG = -1e30
VMEM_LIMIT = 52 * 1024 * 1024
HIGHEST = lax.Precision.HIGHEST
NT = (((1,), (1,)), ((), ()))
TN = (((0,), (0,)), ((), ()))

f32 = jnp.float32
bf16 = jnp.bfloat16


def _params(*sem):
    return pltpu.CompilerParams(dimension_semantics=sem, vmem_limit_bytes=VMEM_LIMIT)


def _tile(n, prefs):
    for p in prefs:
        if n % p == 0:
            return p
    raise ValueError(f"no tile in {prefs} divides {n}")


LANES = 128


def _ln_math(z, w, b):
    mu = jnp.mean(z, axis=-1, keepdims=True)
    d = z - mu
    var = jnp.mean(d * d, axis=-1, keepdims=True)
    return d * lax.rsqrt(var + LN_EPS) * w + b


def _small_proj(h, pw_ref, pb_ref):
    h_hi = h.astype(bf16)
    h_lo = (h - h_hi.astype(f32)).astype(bf16)
    lhs = jnp.concatenate([h_hi, h_hi, h_lo], axis=1)
    return jnp.dot(lhs, pw_ref[...], preferred_element_type=f32) + pb_ref[...]


def _rows_to_slabs(ref, rows):
    n, d = rows.shape
    ns = d // LANES
    for c in range(ns):
        ref[pl.ds(c, n, stride=ns), :] = rows[:, c * LANES:(c + 1) * LANES]


def _slab_column(ref, lead, first_token, n, ns, c):
    return ref[lead + (pl.ds(first_token * ns + c, n, stride=ns), slice(None))]


def _embed_ln_kernel(x_ref, w_ref, b_ref, pw_ref, pb_ref, h_ref, hb_ref, p_ref):
    h = _ln_math(x_ref[...], w_ref[...], b_ref[...])
    h_ref[...] = h
    hb_ref[...] = h.astype(bf16)
    p_ref[...] = _small_proj(h, pw_ref, pb_ref)


def _embed_layer_norm(x, w, b, pw, pb):
    t, d = x.shape
    tm = _tile(t, (256, 128, 64))
    row = pl.BlockSpec((tm, d), lambda i: (i, 0))
    vec = pl.BlockSpec((1, d), lambda i: (0, 0))
    small = pl.BlockSpec((tm, SMALL_N), lambda i: (i, 0))
    return pl.pallas_call(
        _embed_ln_kernel, grid=(t // tm,),
        in_specs=[row, vec, vec, pl.BlockSpec((3 * d, SMALL_N), lambda i: (0, 0)),
                  pl.BlockSpec((1, SMALL_N), lambda i: (0, 0))],
        out_specs=[row, row, small],
        out_shape=[jax.ShapeDtypeStruct((t, d), f32), jax.ShapeDtypeStruct((t, d), bf16),
                   jax.ShapeDtypeStruct((t, SMALL_N), f32)],
        compiler_params=_params("parallel"), name="embed_layer_norm",
    )(x, w.reshape(1, d), b.reshape(1, d), pw, pb.reshape(1, SMALL_N))


def _mixer_ln_kernel(x_ref, mw_ref, res_ref, w_ref, b_ref, pw_ref, pb_ref, hs_ref, te_ref, tg_ref, *, alpha, n_exp):
    z = alpha * res_ref[...] + jnp.dot(x_ref[...], mw_ref[...], preferred_element_type=f32)
    h = _ln_math(z, w_ref[...], b_ref[...])
    _rows_to_slabs(hs_ref, h)
    lane = lax.broadcasted_iota(jnp.int32, (1, SMALL_N), 1)
    logits = jnp.where(lane < n_exp, _small_proj(h, pw_ref, pb_ref), NEG)
    top_e = jnp.zeros(logits.shape, jnp.int32)
    vals = []
    for kk in range(TOP_K):
        m = jnp.max(logits, axis=-1, keepdims=True)
        idx = jnp.min(jnp.where(logits == m, lane, SMALL_N), axis=-1, keepdims=True)
        top_e = jnp.where(lane == kk, idx, top_e)
        vals.append(m)
        logits = jnp.where(lane == idx, NEG, logits)
    ex = [jnp.exp(v - vals[0]) for v in vals]
    den = ex[0]
    for e_k in ex[1:]:
        den = den + e_k
    gate = jnp.zeros(logits.shape, f32)
    for kk in range(TOP_K):
        gate = jnp.where(lane == kk, ex[kk] / den, gate)
    te_ref[...] = top_e
    tg_ref[...] = gate


def _mixer_layer_norm(x, mm_w, res, alpha, w, b, pw, pb, n_exp):
    t, k = x.shape
    d = w.shape[0]
    ns = d // LANES
    tm = _tile(t, (256, 128, 64))
    row = pl.BlockSpec((tm, d), lambda i: (i, 0))
    vec = pl.BlockSpec((1, d), lambda i: (0, 0))
    small = pl.BlockSpec((tm, SMALL_N), lambda i: (i, 0))
    return pl.pallas_call(
        functools.partial(_mixer_ln_kernel, alpha=alpha, n_exp=n_exp), grid=(t // tm,),
        in_specs=[pl.BlockSpec((tm, k), lambda i: (i, 0)), pl.BlockSpec((k, d), lambda i: (0, 0)), row, vec, vec,
                  pl.BlockSpec((3 * d, SMALL_N), lambda i: (0, 0)), pl.BlockSpec((1, SMALL_N), lambda i: (0, 0))],
        out_specs=[pl.BlockSpec((tm * ns, LANES), lambda i: (i, 0)), small, small],
        out_shape=[jax.ShapeDtypeStruct((t * ns, LANES), f32), jax.ShapeDtypeStruct((t, SMALL_N), jnp.int32),
                   jax.ShapeDtypeStruct((t, SMALL_N), f32)],
        compiler_params=_params("parallel"), name="mixer_layer_norm",
    )(x, mm_w, res, w.reshape(1, d), b.reshape(1, d), pw, pb.reshape(1, SMALL_N))


def _mm_kernel(x_ref, w_ref, o_ref):
    o_ref[...] = jnp.dot(x_ref[...], w_ref[...], preferred_element_type=f32).astype(o_ref.dtype)


def _matmul(x, w, out_dtype, name):
    m, k = x.shape
    n = w.shape[1]
    tm = _tile(m, (768, 512, 384, 256, 128))
    tn = _tile(n, (1024, 512, 256, 128))
    return pl.pallas_call(
        _mm_kernel,
        grid=(n // tn, m // tm),
        in_specs=[pl.BlockSpec((tm, k), lambda j, i: (i, 0)), pl.BlockSpec((k, tn), lambda j, i: (0, j))],
        out_specs=pl.BlockSpec((tm, tn), lambda j, i: (i, j)),
        out_shape=jax.ShapeDtypeStruct((m, n), out_dtype),
        compiler_params=_params("parallel", "parallel"), name=name,
    )(x, w)


LANE_PAIR = 256


def _deinterleave_kernel(x_ref, p_ref, o_ref):
    half = o_ref.shape[-1] // 2
    out_w = LANE_PAIR // 2
    for j in range(x_ref.shape[-1] // LANE_PAIR):
        xb = x_ref[0, :, j * LANE_PAIR:(j + 1) * LANE_PAIR].astype(bf16)
        y = jnp.dot(xb, p_ref[...], preferred_element_type=f32).astype(bf16)
        o_ref[0, :, j * out_w:(j + 1) * out_w] = y[:, :out_w]
        o_ref[0, :, half + j * out_w:half + (j + 1) * out_w] = y[:, out_w:]


def _deinterleave_cast(w):
    n, d, f2 = w.shape
    assert f2 % LANE_PAIR == 0
    tr = _tile(d, (1024, 512, 256, 128))
    perm = np.zeros((LANE_PAIR, LANE_PAIR), np.float32)
    idx = np.arange(LANE_PAIR // 2)
    perm[2 * idx, idx] = 1.0
    perm[2 * idx + 1, LANE_PAIR // 2 + idx] = 1.0
    return pl.pallas_call(
        _deinterleave_kernel,
        grid=(n, d // tr),
        in_specs=[pl.BlockSpec((1, tr, f2), lambda i, r: (i, r, 0)),
                  pl.BlockSpec((LANE_PAIR, LANE_PAIR), lambda i, r: (0, 0))],
        out_specs=pl.BlockSpec((1, tr, f2), lambda i, r: (i, r, 0)),
        out_shape=jax.ShapeDtypeStruct((n, d, f2), bf16),
        compiler_params=_params("parallel", "parallel"), name="deinterleave_cast",
    )(w, jnp.asarray(perm, bf16))


def _split_lhs(a):
    hi = a.astype(bf16)
    lo = (a - hi.astype(f32)).astype(bf16)
    return jnp.concatenate([hi, hi, lo], axis=1)


def _split_rhs(w):
    hi = w.astype(bf16)
    lo = (w - hi.astype(f32)).astype(bf16)
    return jnp.concatenate([hi, lo, hi], axis=-2)


def _branch_kernel(oa_ref, ob_ref, wa_ref, wb_ref, ga_ref, gb_ref, o_ref):
    ya = jnp.dot(oa_ref[...], wa_ref[...], preferred_element_type=f32)
    yb = jnp.dot(ob_ref[...], wb_ref[...], preferred_element_type=f32)
    mixed = jax.nn.sigmoid(ga_ref[...].astype(f32)) * ya + jax.nn.sigmoid(gb_ref[...].astype(f32)) * yb
    o_ref[...] = mixed.astype(o_ref.dtype)


def _branch_merge(o_a, o_b, w_a, w_b, proj, ga_col, gb_col):
    t, ka = o_a.shape
    kb = o_b.shape[1]
    d = w_a.shape[1]
    tm = _tile(t, (768, 512, 384, 256, 128))
    tn = 512
    assert ga_col % tn == 0 and gb_col % tn == 0 and d % tn == 0
    return pl.pallas_call(
        _branch_kernel,
        grid=(d // tn, t // tm),
        in_specs=[
            pl.BlockSpec((tm, ka), lambda j, i: (i, 0)),
            pl.BlockSpec((tm, kb), lambda j, i: (i, 0)),
            pl.BlockSpec((ka, tn), lambda j, i: (0, j)),
            pl.BlockSpec((kb, tn), lambda j, i: (0, j)),
            pl.BlockSpec((tm, tn), lambda j, i: (i, ga_col // tn + j)),
            pl.BlockSpec((tm, tn), lambda j, i: (i, gb_col // tn + j)),
        ],
        out_specs=pl.BlockSpec((tm, tn), lambda j, i: (i, j)),
        out_shape=jax.ShapeDtypeStruct((t, d), bf16),
        compiler_params=_params("parallel", "parallel"), name="branch_merge",
    )(o_a, o_b, w_a, w_b, proj, proj)


N_LEVELS = 6
DIAG_LEVEL = N_LEVELS + 1
KDEC_BLOCK = N_LEVELS + 1


def _gla_constants(reverse):
    c = CHUNK
    i = np.arange(c)[:, None]
    s = np.arange(c)[None, :]
    blocks = [(s >= i) if reverse else (s <= i)]
    level = np.zeros((c, c), np.int32)
    size = c
    for lv in range(1, N_LEVELS + 1):
        half = size // 2
        m = np.zeros((c, c), bool)
        for r in range(c):
            base = (r // size) * size
            first_half = (r % size) < half
            if not reverse:
                mid = base + half - 1
                if first_half:
                    m[r, r + 1:mid + 1] = True
                else:
                    m[r, mid + 1:r + 1] = True
            else:
                mid = base + half
                if first_half:
                    m[r, r:mid] = True
                else:
                    m[r, mid:r] = True
        blocks.append(m)
        same = (i // size) == (s // size)
        if not reverse:
            cond = same & ((i % size) >= half) & ((s % size) < half)
        else:
            cond = same & ((i % size) < half) & ((s % size) >= half)
        level[cond] = lv
        size = half
    level[np.eye(c, dtype=bool)] = DIAG_LEVEL
    blocks.append((s < i) if reverse else (s > i))
    return np.concatenate(blocks, axis=0).astype(np.float32), level


def _gla_kernel(*refs, reverse, cps, final, dk):
    it = iter(refs)
    q_ref, k_ref, v_ref, lr_ref, gw_ref, gb_ref, ms_ref, lvl_ref = (next(it) for _ in range(8))
    if final:
        r_ref, ob_ref, nw_ref = next(it), next(it), next(it)
    o_ref = next(it)
    st_ref = next(it)
    step = pl.program_id(2)
    nsteps = pl.num_programs(2)

    @pl.when(step == 0)
    def _():
        st_ref[...] = jnp.zeros_like(st_ref)

    blk = (nsteps - 1 - step) if reverse else step
    ms = ms_ref[...]
    lvl = lvl_ref[...]
    row = lax.broadcasted_iota(jnp.int32, (CHUNK, 1), 0)
    scale = dk ** -0.5
    order = list(range(cps))[::-1] if reverse else list(range(cps))
    rows = [slice(c * CHUNK, (c + 1) * CHUNK) for c in range(cps)]

    q, k, v, e = {}, {}, {}, {}
    for c in order:
        valid = (row >= PAD) | (blk * cps + c > 0)
        q[c] = jnp.where(valid, q_ref[rows[c], :].astype(f32) * scale, 0.0)
        k[c] = jnp.where(valid, k_ref[rows[c], :].astype(f32), 0.0)
        v[c] = jnp.where(valid, v_ref[rows[c], :], jnp.zeros((), bf16))
        x = jnp.dot(lr_ref[rows[c], :], gw_ref[...], preferred_element_type=f32) + gb_ref[...]
        g = (jnp.minimum(x, 0.0) - jnp.log1p(jnp.exp(-jnp.abs(x)))) * (1.0 / GATE_NORMALIZER)
        g = jnp.where(valid, g, 0.0)
        g_hi = g.astype(bf16)
        g_lo = (g - g_hi.astype(f32)).astype(bf16)
        e[c] = jnp.dot(ms, jnp.concatenate([g_hi, g_lo], axis=0), preferred_element_type=f32)
    scores = {c: jnp.zeros((CHUNK, CHUNK), f32) for c in order}
    for lv in range(1, N_LEVELS + 1):
        for c in order:
            w = jnp.exp(e[c][lv * CHUNK:(lv + 1) * CHUNK])
            a = lax.dot_general((q[c] * w).astype(bf16), (k[c] * w).astype(bf16), NT, preferred_element_type=f32)
            scores[c] = jnp.where(lvl == lv, a, scores[c])
    qi, upd, total, o_intra = {}, {}, {}, {}
    for c in order:
        a = lax.dot_general(q[c].astype(bf16), k[c].astype(bf16), NT, preferred_element_type=f32)
        scores[c] = jnp.where(lvl == DIAG_LEVEL, a, scores[c])
        cum = e[c][0:CHUNK]
        qi[c] = (q[c] * jnp.exp(cum)).astype(bf16)
        kd = (k[c] * jnp.exp(e[c][KDEC_BLOCK * CHUNK:(KDEC_BLOCK + 1) * CHUNK])).astype(bf16)
        upd[c] = lax.dot_general(v[c], kd, TN, preferred_element_type=f32)
        total[c] = jnp.exp(cum[0:1] if reverse else cum[CHUNK - 1:CHUNK])
    for c in order:
        o_intra[c] = jnp.dot(scores[c].astype(bf16), v[c], preferred_element_type=f32)

    st = st_ref[...]
    for c in order:
        o = lax.dot_general(qi[c], st.astype(bf16), NT, preferred_element_type=f32) + o_intra[c]
        st = st * total[c] + upd[c]
        if final:
            o = o + ob_ref[rows[c], :]
            o = o * lax.rsqrt(jnp.mean(o * o, axis=-1, keepdims=True) + RMS_EPS) * nw_ref[...]
            o = o * jax.nn.silu(r_ref[rows[c], :].astype(f32))
        o_ref[rows[c], :] = o.astype(o_ref.dtype)
    st_ref[...] = st


def _gla_direction(proj, lr, gw, gb, bsz, lp, reverse, final_args=None):
    t = proj.shape[0]
    heads = GLA_HEADS
    dk = gw.shape[1] // heads
    dv = 2 * dk
    n_chunks = lp // CHUNK
    cps = _tile(n_chunks, (3, 2, 1))
    rb = cps * CHUNK
    nsteps = n_chunks // cps
    final = final_args is not None
    ms, lvl = _gla_constants(reverse)
    ms = np.concatenate([ms, ms], axis=1)
    nblk = ms.shape[0]
    rank3 = lr.shape[1]

    def rblk(b, s):
        return b * nsteps + ((nsteps - 1 - s) if reverse else s)

    k_off = heads
    v_off = heads
    r_off = 2 * heads
    in_specs = [
        pl.BlockSpec((rb, dk), lambda b, h, s: (rblk(b, s), h)),
        pl.BlockSpec((rb, dk), lambda b, h, s: (rblk(b, s), k_off + h)),
        pl.BlockSpec((rb, dv), lambda b, h, s: (rblk(b, s), v_off + h)),
        pl.BlockSpec((rb, rank3), lambda b, h, s: (rblk(b, s), 0)),
        pl.BlockSpec((rank3, dk), lambda b, h, s: (0, h)),
        pl.BlockSpec((1, dk), lambda b, h, s: (0, h)),
        pl.BlockSpec((nblk, 2 * CHUNK), lambda b, h, s: (0, 0)),
        pl.BlockSpec((CHUNK, CHUNK), lambda b, h, s: (0, 0)),
    ]
    ins = [proj, proj, proj, lr, gw, gb.reshape(1, -1), jnp.asarray(ms, bf16), jnp.asarray(lvl)]
    if final:
        o_other, nw = final_args
        in_specs += [
            pl.BlockSpec((rb, dv), lambda b, h, s: (rblk(b, s), r_off + h)),
            pl.BlockSpec((rb, dv), lambda b, h, s: (rblk(b, s), h)),
            pl.BlockSpec((1, dv), lambda b, h, s: (0, 0)),
        ]
        ins += [proj, o_other, nw.reshape(1, dv)]
    return pl.pallas_call(
        functools.partial(_gla_kernel, reverse=reverse, cps=cps, final=final, dk=dk),
        grid=(bsz, heads, nsteps),
        in_specs=in_specs,
        out_specs=pl.BlockSpec((rb, dv), lambda b, h, s: (rblk(b, s), h)),
        out_shape=jax.ShapeDtypeStruct((t, heads * dv), bf16 if final else f32),
        scratch_shapes=[pltpu.VMEM((dv, dk), f32)],
        compiler_params=_params("parallel", "parallel", "arbitrary"),
        name="gla_final" if final else "gla_partial",
    )(*ins)


N_VARIANTS = WIN_H


def _na_bias_tables(rpb, grid_rows):
    del grid_rows
    n_layers, heads = rpb.shape[0], rpb.shape[1]
    cols = np.arange(GRID_W)
    col_start = np.clip(cols - WIN_W // 2, 0, GRID_W - WIN_W)
    kc = np.arange(GRID_W)
    in_win = (kc[None, :] >= col_start[:, None]) & (kc[None, :] < col_start[:, None] + WIN_W)
    dc = np.clip(kc[None, :] - cols[:, None] + (WIN_W - 1), 0, 2 * WIN_W - 2)
    rpb_c = jnp.where(in_win[None, None, None], rpb[:, :, :, dc], NEG)
    tbl = jnp.stack([rpb_c[:, :, vi:vi + WIN_H] for vi in range(N_VARIANTS)], axis=2)
    tbl = tbl.reshape(n_layers, heads // 2, 2, N_VARIANTS, WIN_H, GRID_W, GRID_W)
    tbl = jnp.transpose(tbl, (0, 1, 3, 2, 5, 4, 6))
    return tbl.reshape(n_layers, heads // 2, N_VARIANTS, 2 * GRID_W, WIN_H * GRID_W)


def _na_kernel(q_ref, k_ref, v_ref, bias_ref, mb_ref, o_ref, *, bps, grid_rows):
    step = pl.program_id(2)
    lane = lax.broadcasted_iota(jnp.int32, (1, 2 * NA_HEAD_DIM), 1)
    head0 = lane < NA_HEAD_DIM
    km = k_ref[PAD:FRONT, :]
    vm = v_ref[PAD:FRONT, :]
    mb = mb_ref[0]
    mbias = jnp.concatenate([jnp.broadcast_to(mb[0:1], (GRID_W, N_META)),
                             jnp.broadcast_to(mb[1:2], (GRID_W, N_META))], axis=0)
    zero = jnp.zeros((), bf16)
    scale = NA_HEAD_DIM ** -0.5

    def stacked_queries(j):
        q2 = q_ref[j * GRID_W:(j + 1) * GRID_W, :] * jnp.asarray(scale, bf16)
        return jnp.concatenate([jnp.where(head0, q2, zero), jnp.where(head0, zero, q2)], axis=0)

    def unstack(o):
        return jnp.where(head0, o[0:GRID_W], o[GRID_W:2 * GRID_W])

    row = lax.broadcasted_iota(jnp.int32, (GRID_W, 1), 0)
    blocks = range(bps)
    qq, vv, sl, sm, front = {}, {}, {}, {}, {}
    for j in blocks:
        g = step * bps + j
        r = g - 1
        r0 = jnp.clip(r - WIN_H // 2, 0, grid_rows - WIN_H)
        vi = jnp.clip(r0 - r + WIN_H // 2 + 3, 0, N_VARIANTS - 1)
        ks = pl.multiple_of((r0 + 1) * GRID_W, GRID_W)
        kk = k_ref[pl.ds(ks, WIN_H * GRID_W), :]
        vv[j] = v_ref[pl.ds(ks, WIN_H * GRID_W), :]
        qq[j] = stacked_queries(j)
        front[j] = g == 0
        sl[j] = lax.dot_general(qq[j], kk, NT, preferred_element_type=f32) + bias_ref[0, vi]
        sm[j] = lax.dot_general(qq[j], km, NT, preferred_element_type=f32) + mbias
    p_loc, p_meta, den = {}, {}, {}
    for j in blocks:
        s_loc = sl[j] + jnp.where(front[j], NEG, 0.0) if j == 0 else sl[j]
        mx = jnp.maximum(jnp.max(s_loc, axis=-1, keepdims=True), jnp.max(sm[j], axis=-1, keepdims=True))
        p_loc[j] = jnp.exp(s_loc - mx)
        p_meta[j] = jnp.exp(sm[j] - mx)
        den[j] = jnp.sum(p_loc[j], axis=-1, keepdims=True) + jnp.sum(p_meta[j], axis=-1, keepdims=True)
    for j in blocks:
        o = (jnp.dot(p_loc[j].astype(bf16), vv[j], preferred_element_type=f32)
             + jnp.dot(p_meta[j].astype(bf16), vm, preferred_element_type=f32))
        o = unstack(o / den[j])
        if j == 0:
            o = jnp.where((row >= PAD) | jnp.logical_not(front[j]), o, 0.0)
        o_ref[j * GRID_W:(j + 1) * GRID_W, :] = o.astype(o_ref.dtype)


def _neighborhood_attention(proj, bias_tbl, meta_bias, bsz, lp, q_col):
    t = proj.shape[0]
    n_pairs = bias_tbl.shape[0]
    width = 2 * NA_HEAD_DIM
    grid_rows = (lp - FRONT) // GRID_W
    assert grid_rows >= WIN_H
    n_blocks = lp // GRID_W
    bps = _tile(n_blocks, (3, 2, 1))
    nsteps = n_blocks // bps
    qb = q_col // width
    return pl.pallas_call(
        functools.partial(_na_kernel, bps=bps, grid_rows=grid_rows),
        grid=(bsz, n_pairs, nsteps),
        in_specs=[
            pl.BlockSpec((bps * GRID_W, width), lambda b, p, s: (b * nsteps + s, qb + p)),
            pl.BlockSpec((lp, width), lambda b, p, s: (b, qb + n_pairs + p)),
            pl.BlockSpec((lp, width), lambda b, p, s: (b, qb + 2 * n_pairs + p)),
            pl.BlockSpec((1,) + bias_tbl.shape[1:], lambda b, p, s: (p, 0, 0, 0)),
            pl.BlockSpec((1, 2, N_META), lambda b, p, s: (p, 0, 0)),
        ],
        out_specs=pl.BlockSpec((bps * GRID_W, width), lambda b, p, s: (b * nsteps + s, p)),
        out_shape=jax.ShapeDtypeStruct((t, n_pairs * width), bf16),
        compiler_params=_params("parallel", "parallel", "arbitrary"),
        name="neighborhood_attention",
    )(proj, proj, proj, bias_tbl, meta_bias.reshape(n_pairs, 2, N_META))


ISSUE_UNROLL = 8


def _start_row_gather(src_hbm, idx_smem, slot, dst, sem, n, ns):
    base = slot * n

    def body(j, c):
        r0 = j * ISSUE_UNROLL
        for u in range(ISSUE_UNROLL):
            src_row = pl.multiple_of(idx_smem[base + r0 + u] * ns, ns)
            dst_row = pl.multiple_of((r0 + u) * ns, ns)
            pltpu.make_async_copy(src_hbm.at[pl.ds(src_row, ns), :], dst.at[slot, pl.ds(dst_row, ns), :],
                                  sem.at[slot]).start()
        return c

    lax.fori_loop(0, n // ISSUE_UNROLL, body, 0)


def _wait_row_gather(src_hbm, slot, dst, sem, n, ns):
    pltpu.make_async_copy(src_hbm.at[pl.ds(0, n * ns), :], dst.at[slot], sem.at[slot]).wait()


def _gather_pipeline(idx_hbm, src_hbm, idx_smem, buf, sem_idx, sem_rows, n, ns):
    g = pl.program_id(0)
    last = pl.num_programs(0) - 1
    slot = g % 2
    nxt = 1 - slot

    def idx_copy(step, s):
        return pltpu.make_async_copy(idx_hbm.at[step], idx_smem.at[pl.ds(s * n, n)], sem_idx.at[s])

    @pl.when(g == 0)
    def _():
        c = idx_copy(0, 0)
        c.start()
        c.wait()
        _start_row_gather(src_hbm, idx_smem, 0, buf, sem_rows, n, ns)

        @pl.when(last >= 1)
        def _():
            idx_copy(1, 1).start()

    @pl.when(g < last)
    def _():
        idx_copy(g + 1, nxt).wait()
        _start_row_gather(src_hbm, idx_smem, nxt, buf, sem_rows, n, ns)

    @pl.when(g + 1 < last)
    def _():
        idx_copy(g + 2, slot).start()

    _wait_row_gather(src_hbm, slot, buf, sem_rows, n, ns)
    return slot


def _expert_kernel(ge_ref, nu_ref, idx_hbm, x_hbm, w1_ref, b1_ref, w2_ref, b2_ref, rw_ref, y_ref,
                   idx_smem, xbuf, sem_idx, sem_rows):
    del ge_ref
    ns = xbuf.shape[1] // GROUP_ROWS
    slot = _gather_pipeline(idx_hbm, x_hbm, idx_smem, xbuf, sem_idx, sem_rows, GROUP_ROWS, ns)
    g = pl.program_id(0)
    ff = w2_ref.shape[1]

    @pl.when(g < nu_ref[0])
    def _():
        x = jnp.concatenate([_slab_column(xbuf, (slot,), 0, GROUP_ROWS, ns, c) for c in range(ns)], axis=1).astype(bf16)
        h = jnp.dot(x, w1_ref[0], preferred_element_type=f32) + b1_ref[0]
        x_glu = jnp.minimum(h[:, :ff], SWIGLU_LIMIT)
        x_lin = jnp.clip(h[:, ff:], -SWIGLU_LIMIT, SWIGLU_LIMIT)
        act = x_glu * jax.nn.sigmoid(SWIGLU_ALPHA * x_glu) * (x_lin + 1.0)
        y = jnp.dot(act.astype(bf16), w2_ref[0], preferred_element_type=f32) + b2_ref[0]
        _rows_to_slabs(y_ref, y * rw_ref[...])

    @pl.when(g >= nu_ref[0])
    def _():
        y_ref[...] = jnp.zeros_like(y_ref)


def _expert_groups(hs, row_tok, row_w, group_e, n_used, w1, b1, w2, b2):
    d = w1.shape[1]
    nslab = d // LANES
    n_groups = row_tok.shape[0]
    n_exp, _, f2 = w1.shape
    ff = f2 // 2
    grid_spec = pltpu.PrefetchScalarGridSpec(
        num_scalar_prefetch=2,
        grid=(n_groups,),
        in_specs=[
            pl.BlockSpec(memory_space=pl.ANY),
            pl.BlockSpec(memory_space=pl.ANY),
            pl.BlockSpec((1, d, f2), lambda g, ge, nu: (ge[g], 0, 0)),
            pl.BlockSpec((1, 1, f2), lambda g, ge, nu: (ge[g], 0, 0)),
            pl.BlockSpec((1, ff, d), lambda g, ge, nu: (ge[g], 0, 0)),
            pl.BlockSpec((1, 1, d), lambda g, ge, nu: (ge[g], 0, 0)),
            pl.BlockSpec((GROUP_ROWS, 1), lambda g, ge, nu: (g, 0)),
        ],
        out_specs=pl.BlockSpec((GROUP_ROWS * nslab, LANES), lambda g, ge, nu: (g, 0)),
        scratch_shapes=[
            pltpu.SMEM((2 * GROUP_ROWS,), jnp.int32),
            pltpu.VMEM((2, GROUP_ROWS * nslab, LANES), f32),
            pltpu.SemaphoreType.DMA((2,)),
            pltpu.SemaphoreType.DMA((2,)),
        ],
    )
    return pl.pallas_call(
        _expert_kernel, grid_spec=grid_spec,
        out_shape=jax.ShapeDtypeStruct((n_groups * GROUP_ROWS * nslab, LANES), f32),
        compiler_params=_params("arbitrary"), name="expert_groups",
    )(group_e, n_used, row_tok, hs, w1, b1.reshape(n_exp, 1, f2), w2, b2.reshape(n_exp, 1, d),
      row_w.reshape(n_groups * GROUP_ROWS, 1))


COMBINE_TOKENS = 128


def _combine_ln_kernel(*refs, alpha, has_proj):
    it = iter(refs)
    idx_hbm, y_hbm, res_ref, w_ref, b_ref = (next(it) for _ in range(5))
    pw_ref, pb_ref = (next(it), next(it)) if has_proj else (None, None)
    h_ref, hb_ref = next(it), next(it)
    p_ref = next(it) if has_proj else None
    idx_smem, buf, sem_idx, sem_rows = (next(it) for _ in range(4))
    tc = COMBINE_TOKENS
    ns = res_ref.shape[0] // tc
    slot = _gather_pipeline(idx_hbm, y_hbm, idx_smem, buf, sem_idx, sem_rows, TOP_K * tc, ns)
    cols = []
    for c in range(ns):
        acc = _slab_column(buf, (slot,), 0, tc, ns, c)
        for kk in range(1, TOP_K):
            acc = acc + _slab_column(buf, (slot,), kk * tc, tc, ns, c)
        cols.append(alpha * _slab_column(res_ref, (), 0, tc, ns, c) + acc)
    h = _ln_math(jnp.concatenate(cols, axis=1), w_ref[...], b_ref[...])
    h_ref[...] = h
    hb_ref[...] = h.astype(bf16)
    if has_proj:
        p_ref[...] = _small_proj(h, pw_ref, pb_ref)


def _combine_layer_norm(y_rows, dest, hs, alpha, w, b, pw=None, pb=None):
    d = w.shape[0]
    nslab = d // LANES
    t = hs.shape[0] // nslab
    tc = COMBINE_TOKENS
    n_tiles = t // tc
    has_proj = pw is not None
    idx = dest.reshape(n_tiles, tc, TOP_K).transpose(0, 2, 1).reshape(n_tiles, TOP_K * tc)
    row = pl.BlockSpec((tc, d), lambda i: (i, 0))
    vec = pl.BlockSpec((1, d), lambda i: (0, 0))
    ins = [idx, y_rows, hs, w.reshape(1, d), b.reshape(1, d)]
    in_specs = [pl.BlockSpec(memory_space=pl.ANY), pl.BlockSpec(memory_space=pl.ANY),
                pl.BlockSpec((tc * nslab, LANES), lambda i: (i, 0)), vec, vec]
    outs = [jax.ShapeDtypeStruct((t, d), f32), jax.ShapeDtypeStruct((t, d), bf16)]
    out_specs = [row, row]
    if has_proj:
        ins += [pw, pb.reshape(1, SMALL_N)]
        in_specs += [pl.BlockSpec((3 * d, SMALL_N), lambda i: (0, 0)), pl.BlockSpec((1, SMALL_N), lambda i: (0, 0))]
        outs.append(jax.ShapeDtypeStruct((t, SMALL_N), f32))
        out_specs.append(pl.BlockSpec((tc, SMALL_N), lambda i: (i, 0)))
    return pl.pallas_call(
        functools.partial(_combine_ln_kernel, alpha=alpha, has_proj=has_proj),
        grid=(n_tiles,), in_specs=in_specs, out_specs=out_specs, out_shape=outs,
        scratch_shapes=[
            pltpu.SMEM((2 * TOP_K * tc,), jnp.int32),
            pltpu.VMEM((2, TOP_K * tc * nslab, LANES), f32),
            pltpu.SemaphoreType.DMA((2,)),
            pltpu.SemaphoreType.DMA((2,)),
        ],
        compiler_params=_params("arbitrary"), name="combine_layer_norm",
    )(*ins)


def _route(top_e, gate, n_exp):
    t = top_e.shape[0]
    n = t * TOP_K
    flat_e = top_e.reshape(-1)
    pos = jnp.arange(n, dtype=jnp.int32)
    e_sorted, order = lax.sort((flat_e, pos), num_keys=1, is_stable=True)
    _, inv = lax.sort((order, pos), num_keys=1)
    bounds = jnp.searchsorted(e_sorted, jnp.arange(n_exp + 1, dtype=jnp.int32), side="left",
                              method="scan_unrolled").astype(jnp.int32)
    start = bounds[:n_exp]
    counts = bounds[1:] - start
    padded = (counts + GROUP_ROWS - 1) // GROUP_ROWS * GROUP_ROWS
    pend = jnp.cumsum(padded)
    pstart = pend - padded
    dest_sorted = pstart[e_sorted] + (pos - start[e_sorted])
    dest = dest_sorted[inv].reshape(t, TOP_K)
    n_groups = -(-n // GROUP_ROWS) + n_exp
    group_row0 = jnp.arange(n_groups, dtype=jnp.int32) * GROUP_ROWS
    group_e = jnp.minimum(jnp.sum(pend[None, :] <= group_row0[:, None], axis=1, dtype=jnp.int32), n_exp - 1)
    rows = jnp.arange(n_groups * GROUP_ROWS, dtype=jnp.int32)
    row_e = jnp.repeat(group_e, GROUP_ROWS)
    within = rows - pstart[row_e]
    live = (within < counts[row_e]) & (rows < pend[n_exp - 1])
    src = jnp.clip(start[row_e] + within, 0, n - 1)
    row_tok = jnp.where(live, order[src] // TOP_K, 0).astype(jnp.int32)
    row_w = jnp.where(live, gate.reshape(-1)[order[src]], 0.0)
    n_used = (pend[n_exp - 1] // GROUP_ROWS).astype(jnp.int32).reshape(1)
    return (row_tok.reshape(n_groups, GROUP_ROWS), row_w.reshape(n_groups, GROUP_ROWS), group_e, n_used, dest)


def kernel(x, meta_tokens, ln0_w, ln0_b, w_in, gk_fwd_w, gk_fwd_b, gk_bwd_w, gk_bwd_b, gla_norm_w, w_branch_a, rpb, meta_bias, w_branch_b, w_out, ln1_w, ln1_b, router_w, router_b, w1, b1, w2, b2, ln2_w, ln2_b):
    bsz, seq, d = x.shape
    depth = w_in.shape[0]
    alpha = (2 * depth) ** 0.25
    lp = FRONT + seq
    kdim = gk_fwd_w.shape[2]
    vdim = w_branch_a.shape[1]
    ndim = w_branch_b.shape[1]
    n_exp = router_w.shape[2]
    ff = w2.shape[2]
    assert seq % GRID_W == 0 and n_exp <= SMALL_N and 2 * GATE_RANK <= SMALL_N

    lr_col = 2 * kdim + 2 * vdim
    nq_col = lr_col
    ga_col = nq_col + 3 * ndim
    gb_col = ga_col + d
    w_main = jnp.concatenate([w_in[:, :, :lr_col], w_in[:, :, lr_col + 2 * GATE_RANK:]], axis=2).astype(bf16)
    w_lr = _split_rhs(jnp.pad(w_in[:, :, lr_col:lr_col + 2 * GATE_RANK], ((0, 0), (0, 0), (0, SMALL_N - 2 * GATE_RANK))))
    w_rt = _split_rhs(jnp.pad(router_w, ((0, 0), (0, 0), (0, SMALL_N - n_exp))))
    b_rt = jnp.pad(router_b, ((0, 0), (0, SMALL_N - n_exp)))
    zeros_small = jnp.zeros((SMALL_N,), f32)
    w_a = w_branch_a.astype(bf16)
    w_b = w_branch_b.astype(bf16)
    w_o = w_out.astype(bf16)
    w1p = _deinterleave_cast(w1.reshape(depth * n_exp, d, 2 * ff)).reshape(depth, n_exp, d, 2 * ff)
    b1p = jnp.concatenate([b1[..., 0::2], b1[..., 1::2]], axis=-1)
    w2b = w2.astype(bf16)
    bias_tbl = _na_bias_tables(rpb, seq // GRID_W)
    gw_f, gw_b = _split_rhs(gk_fwd_w), _split_rhs(gk_bwd_w)

    front = jnp.concatenate([jnp.zeros((PAD, d), x.dtype), meta_tokens.astype(x.dtype)], axis=0)
    xp = jnp.concatenate([jnp.broadcast_to(front[None], (bsz, FRONT, d)), x], axis=1).reshape(bsz * lp, d)
    h, hb, small = _embed_layer_norm(xp, ln0_w, ln0_b, w_lr[0], zeros_small)

    for l in range(depth):
        proj = _matmul(hb, w_main[l], bf16, "in_proj")
        lr_f = _split_lhs(small[:, :GATE_RANK])
        lr_b = _split_lhs(small[:, GATE_RANK:2 * GATE_RANK])
        o_bwd = _gla_direction(proj, lr_b, gw_b[l], gk_bwd_b[l], bsz, lp, reverse=True)
        o_a = _gla_direction(proj, lr_f, gw_f[l], gk_fwd_b[l], bsz, lp, reverse=False,
                             final_args=(o_bwd, gla_norm_w[l]))
        o_b = _neighborhood_attention(proj, bias_tbl[l], meta_bias[l], bsz, lp, nq_col)
        mixed = _branch_merge(o_a, o_b, w_a[l], w_b[l], proj, ga_col, gb_col)
        hs, top_e, gate = _mixer_layer_norm(mixed, w_o[l], h, alpha, ln1_w[l], ln1_b[l], w_rt[l], b_rt[l], n_exp)

        row_tok, row_w, group_e, n_used, dest = _route(top_e[:, :TOP_K], gate[:, :TOP_K], n_exp)
        y_rows = _expert_groups(hs, row_tok, row_w, group_e, n_used, w1p[l], b1p[l], w2b[l], b2[l])
        if l + 1 < depth:
            h, hb, small = _combine_layer_norm(y_rows, dest, hs, alpha, ln2_w[l], ln2_b[l], w_lr[l + 1], zeros_small)
        else:
            h, hb = _combine_layer_norm(y_rows, dest, hs, alpha, ln2_w[l], ln2_b[l])
    del ff, vdim
    return h.reshape(bsz, lp, d)[:, FRONT:]
```

```python
import functools

import numpy as np
import jax
import jax.numpy as jnp
from jax import lax
from jax.experimental import pallas as pl
from jax.experimental.pallas import tpu as pltpu

N_META = 16
GRID_W = 64
GLA_HEADS = 4
GATE_RANK = 16
GATE_NORMALIZER = 16.0
CHUNK = 64
NA_HEAD_DIM = 64
WIN_H = 8
WIN_W = 16
TOP_K = 4
SWIGLU_ALPHA = 1.702
SWIGLU_LIMIT = 7.0
GROUP_ROWS = 256
LN_EPS = 1e-5
RMS_EPS = 1e-6

FRONT = CHUNK
PAD = FRONT - N_META
SMALL_N = 128
NEG = -1e30
VMEM_LIMIT = 52 * 1024 * 1024
HIGHEST = lax.Precision.HIGHEST
NT = (((1,), (1,)), ((), ()))
TN = (((0,), (0,)), ((), ()))

f32 = jnp.float32
bf16 = jnp.bfloat16


def _params(*sem):
    return pltpu.CompilerParams(dimension_semantics=sem, vmem_limit_bytes=VMEM_LIMIT)


def _tile(n, prefs):
    for p in prefs:
        if n % p == 0:
            return p
    raise ValueError(f"no tile in {prefs} divides {n}")


LANES = 128


def _ln_math(z, w, b):
    mu = jnp.mean(z, axis=-1, keepdims=True)
    d = z - mu
    var = jnp.mean(d * d, axis=-1, keepdims=True)
    return d * lax.rsqrt(var + LN_EPS) * w + b


def _small_proj(h, pw_ref, pb_ref):
    h_hi = h.astype(bf16)
    h_lo = (h - h_hi.astype(f32)).astype(bf16)
    lhs = jnp.concatenate([h_hi, h_hi, h_lo], axis=1)
    return jnp.dot(lhs, pw_ref[...], preferred_element_type=f32) + pb_ref[...]


def _rows_to_slabs(ref, rows):
    n, d = rows.shape
    ns = d // LANES
    for c in range(ns):
        ref[pl.ds(c, n, stride=ns), :] = rows[:, c * LANES:(c + 1) * LANES]


def _slab_column(ref, lead, first_token, n, ns, c):
    return ref[lead + (pl.ds(first_token * ns + c, n, stride=ns), slice(None))]


def _embed_ln_kernel(x_ref, w_ref, b_ref, pw_ref, pb_ref, h_ref, hb_ref, p_ref):
    h = _ln_math(x_ref[...], w_ref[...], b_ref[...])
    h_ref[...] = h
    hb_ref[...] = h.astype(bf16)
    p_ref[...] = _small_proj(h, pw_ref, pb_ref)


def _embed_layer_norm(x, w, b, pw, pb):
    t, d = x.shape
    tm = _tile(t, (256, 128, 64))
    row = pl.BlockSpec((tm, d), lambda i: (i, 0))
    vec = pl.BlockSpec((1, d), lambda i: (0, 0))
    small = pl.BlockSpec((tm, SMALL_N), lambda i: (i, 0))
    return pl.pallas_call(
        _embed_ln_kernel, grid=(t // tm,),
        in_specs=[row, vec, vec, pl.BlockSpec((3 * d, SMALL_N), lambda i: (0, 0)),
                  pl.BlockSpec((1, SMALL_N), lambda i: (0, 0))],
        out_specs=[row, row, small],
        out_shape=[jax.ShapeDtypeStruct((t, d), f32), jax.ShapeDtypeStruct((t, d), bf16),
                   jax.ShapeDtypeStruct((t, SMALL_N), f32)],
        compiler_params=_params("parallel"), name="embed_layer_norm",
    )(x, w.reshape(1, d), b.reshape(1, d), pw, pb.reshape(1, SMALL_N))


def _mixer_ln_kernel(x_ref, mw_ref, res_ref, w_ref, b_ref, pw_ref, pb_ref, hs_ref, te_ref, tg_ref, *, alpha, n_exp):
    z = alpha * res_ref[...] + jnp.dot(x_ref[...], mw_ref[...], preferred_element_type=f32)
    h = _ln_math(z, w_ref[...], b_ref[...])
    _rows_to_slabs(hs_ref, h)
    lane = lax.broadcasted_iota(jnp.int32, (1, SMALL_N), 1)
    logits = jnp.where(lane < n_exp, _small_proj(h, pw_ref, pb_ref), NEG)
    top_e = jnp.zeros(logits.shape, jnp.int32)
    vals = []
    for kk in range(TOP_K):
        m = jnp.max(logits, axis=-1, keepdims=True)
        idx = jnp.min(jnp.where(logits == m, lane, SMALL_N), axis=-1, keepdims=True)
        top_e = jnp.where(lane == kk, idx, top_e)
        vals.append(m)
        logits = jnp.where(lane == idx, NEG, logits)
    ex = [jnp.exp(v - vals[0]) for v in vals]
    den = ex[0]
    for e_k in ex[1:]:
        den = den + e_k
    gate = jnp.zeros(logits.shape, f32)
    for kk in range(TOP_K):
        gate = jnp.where(lane == kk, ex[kk] / den, gate)
    te_ref[...] = top_e
    tg_ref[...] = gate


def _mixer_layer_norm(x, mm_w, res, alpha, w, b, pw, pb, n_exp):
    t, k = x.shape
    d = w.shape[0]
    ns = d // LANES
    tm = _tile(t, (256, 128, 64))
    row = pl.BlockSpec((tm, d), lambda i: (i, 0))
    vec = pl.BlockSpec((1, d), lambda i: (0, 0))
    small = pl.BlockSpec((tm, SMALL_N), lambda i: (i, 0))
    return pl.pallas_call(
        functools.partial(_mixer_ln_kernel, alpha=alpha, n_exp=n_exp), grid=(t // tm,),
        in_specs=[pl.BlockSpec((tm, k), lambda i: (i, 0)), pl.BlockSpec((k, d), lambda i: (0, 0)), row, vec, vec,
                  pl.BlockSpec((3 * d, SMALL_N), lambda i: (0, 0)), pl.BlockSpec((1, SMALL_N), lambda i: (0, 0))],
        out_specs=[pl.BlockSpec((tm * ns, LANES), lambda i: (i, 0)), small, small],
        out_shape=[jax.ShapeDtypeStruct((t * ns, LANES), f32), jax.ShapeDtypeStruct((t, SMALL_N), jnp.int32),
                   jax.ShapeDtypeStruct((t, SMALL_N), f32)],
        compiler_params=_params("parallel"), name="mixer_layer_norm",
    )(x, mm_w, res, w.reshape(1, d), b.reshape(1, d), pw, pb.reshape(1, SMALL_N))


def _mm_kernel(x_ref, w_ref, o_ref):
    o_ref[...] = jnp.dot(x_ref[...], w_ref[0], preferred_element_type=f32).astype(o_ref.dtype)


def _matmul(x, w, layer, out_dtype, name):
    m, k = x.shape
    n = w.shape[2]
    tm = _tile(m, (768, 512, 384, 256, 128))
    tn = _tile(n, (1024, 512, 256, 128))
    return pl.pallas_call(
        _mm_kernel,
        grid=(n // tn, m // tm),
        in_specs=[pl.BlockSpec((tm, k), lambda j, i: (i, 0)), pl.BlockSpec((1, k, tn), lambda j, i: (layer, 0, j))],
        out_specs=pl.BlockSpec((tm, tn), lambda j, i: (i, j)),
        out_shape=jax.ShapeDtypeStruct((m, n), out_dtype),
        compiler_params=_params("parallel", "parallel"), name=name,
    )(x, w)


LANE_PAIR = 256


def _deinterleave_kernel(x_ref, p_ref, o_ref):
    half = o_ref.shape[-1] // 2
    out_w = LANE_PAIR // 2
    for j in range(x_ref.shape[-1] // LANE_PAIR):
        xb = x_ref[0, :, j * LANE_PAIR:(j + 1) * LANE_PAIR].astype(bf16)
        y = jnp.dot(xb, p_ref[...], preferred_element_type=f32).astype(bf16)
        o_ref[0, :, j * out_w:(j + 1) * out_w] = y[:, :out_w]
        o_ref[0, :, half + j * out_w:half + (j + 1) * out_w] = y[:, out_w:]


def _deinterleave_cast(w):
    n, d, f2 = w.shape
    assert f2 % LANE_PAIR == 0
    tr = _tile(d, (1024, 512, 256, 128))
    perm = np.zeros((LANE_PAIR, LANE_PAIR), np.float32)
    idx = np.arange(LANE_PAIR // 2)
    perm[2 * idx, idx] = 1.0
    perm[2 * idx + 1, LANE_PAIR // 2 + idx] = 1.0
    return pl.pallas_call(
        _deinterleave_kernel,
        grid=(n, d // tr),
        in_specs=[pl.BlockSpec((1, tr, f2), lambda i, r: (i, r, 0)),
                  pl.BlockSpec((LANE_PAIR, LANE_PAIR), lambda i, r: (0, 0))],
        out_specs=pl.BlockSpec((1, tr, f2), lambda i, r: (i, r, 0)),
        out_shape=jax.ShapeDtypeStruct((n, d, f2), bf16),
        compiler_params=_params("parallel", "parallel"), name="deinterleave_cast",
    )(w, jnp.asarray(perm, bf16))


def _split_lhs(a):
    hi = a.astype(bf16)
    lo = (a - hi.astype(f32)).astype(bf16)
    return jnp.concatenate([hi, hi, lo], axis=1)


def _split_rhs(w):
    hi = w.astype(bf16)
    lo = (w - hi.astype(f32)).astype(bf16)
    return jnp.concatenate([hi, lo, hi], axis=-2)


def _branch_kernel(oa_ref, ob_ref, wa_ref, wb_ref, ga_ref, gb_ref, o_ref):
    ya = jnp.dot(oa_ref[...], wa_ref[...], preferred_element_type=f32)
    yb = jnp.dot(ob_ref[...], wb_ref[...], preferred_element_type=f32)
    mixed = jax.nn.sigmoid(ga_ref[...].astype(f32)) * ya + jax.nn.sigmoid(gb_ref[...].astype(f32)) * yb
    o_ref[...] = mixed.astype(o_ref.dtype)


def _branch_merge(o_a, o_b, w_a, w_b, proj, ga_col, gb_col):
    t, ka = o_a.shape
    kb = o_b.shape[1]
    d = w_a.shape[1]
    tm = _tile(t, (768, 512, 384, 256, 128))
    tn = 512
    assert ga_col % tn == 0 and gb_col % tn == 0 and d % tn == 0
    return pl.pallas_call(
        _branch_kernel,
        grid=(d // tn, t // tm),
        in_specs=[
            pl.BlockSpec((tm, ka), lambda j, i: (i, 0)),
            pl.BlockSpec((tm, kb), lambda j, i: (i, 0)),
            pl.BlockSpec((ka, tn), lambda j, i: (0, j)),
            pl.BlockSpec((kb, tn), lambda j, i: (0, j)),
            pl.BlockSpec((tm, tn), lambda j, i: (i, ga_col // tn + j)),
            pl.BlockSpec((tm, tn), lambda j, i: (i, gb_col // tn + j)),
        ],
        out_specs=pl.BlockSpec((tm, tn), lambda j, i: (i, j)),
        out_shape=jax.ShapeDtypeStruct((t, d), bf16),
        compiler_params=_params("parallel", "parallel"), name="branch_merge",
    )(o_a, o_b, w_a, w_b, proj, proj)


N_LEVELS = 6
DIAG_LEVEL = N_LEVELS + 1
KDEC_BLOCK = N_LEVELS + 1


def _gla_constants(reverse):
    c = CHUNK
    i = np.arange(c)[:, None]
    s = np.arange(c)[None, :]
    blocks = [(s >= i) if reverse else (s <= i)]
    level = np.zeros((c, c), np.int32)
    size = c
    for lv in range(1, N_LEVELS + 1):
        half = size // 2
        m = np.zeros((c, c), bool)
        for r in range(c):
            base = (r // size) * size
            first_half = (r % size) < half
            if not reverse:
                mid = base + half - 1
                if first_half:
                    m[r, r + 1:mid + 1] = True
                else:
                    m[r, mid + 1:r + 1] = True
            else:
                mid = base + half
                if first_half:
                    m[r, r:mid] = True
                else:
                    m[r, mid:r] = True
        blocks.append(m)
        same = (i // size) == (s // size)
        if not reverse:
            cond = same & ((i % size) >= half) & ((s % size) < half)
        else:
            cond = same & ((i % size) < half) & ((s % size) >= half)
        level[cond] = lv
        size = half
    level[np.eye(c, dtype=bool)] = DIAG_LEVEL
    blocks.append((s < i) if reverse else (s > i))
    return np.concatenate(blocks, axis=0).astype(np.float32), level


def _gla_kernel(*refs, reverse, cps, hps, final, dk):
    it = iter(refs)
    q_ref, k_ref, v_ref, lr_ref, gw_ref, gb_ref, ms_ref, lvl_ref = (next(it) for _ in range(8))
    if final:
        r_ref, ob_ref, nw_ref = next(it), next(it), next(it)
    o_ref = next(it)
    st_ref = next(it)
    dv = st_ref.shape[1]
    step = pl.program_id(2)
    nsteps = pl.num_programs(2)

    @pl.when(step == 0)
    def _():
        st_ref[...] = jnp.zeros_like(st_ref)

    blk = (nsteps - 1 - step) if reverse else step
    ms = ms_ref[...]
    lvl = lvl_ref[...]
    row = lax.broadcasted_iota(jnp.int32, (CHUNK, 1), 0)
    scale = dk ** -0.5
    order = list(range(cps))[::-1] if reverse else list(range(cps))
    rows = [slice(c * CHUNK, (c + 1) * CHUNK) for c in range(cps)]
    kcol = [slice(u * dk, (u + 1) * dk) for u in range(hps)]
    vcol = [slice(u * dv, (u + 1) * dv) for u in range(hps)]
    units = [(u, c) for c in order for u in range(hps)]

    q, k, v, e = {}, {}, {}, {}
    for c in order:
        valid = (row >= PAD) | (blk * cps + c > 0)
        x_all = jnp.dot(lr_ref[rows[c], :], gw_ref[...], preferred_element_type=f32) + gb_ref[...]
        for u in range(hps):
            q[u, c] = jnp.where(valid, q_ref[rows[c], kcol[u]].astype(f32) * scale, 0.0)
            k[u, c] = jnp.where(valid, k_ref[rows[c], kcol[u]].astype(f32), 0.0)
            v[u, c] = jnp.where(valid, v_ref[rows[c], vcol[u]], jnp.zeros((), bf16))
            x = x_all[:, kcol[u]]
            g = (jnp.minimum(x, 0.0) - jnp.log1p(jnp.exp(-jnp.abs(x)))) * (1.0 / GATE_NORMALIZER)
            g = jnp.where(valid, g, 0.0)
            g_hi = g.astype(bf16)
            g_lo = (g - g_hi.astype(f32)).astype(bf16)
            e[u, c] = jnp.dot(ms, jnp.concatenate([g_hi, g_lo], axis=0), preferred_element_type=f32)
    scores = {uc: jnp.zeros((CHUNK, CHUNK), f32) for uc in units}
    for lv in range(1, N_LEVELS + 1):
        for uc in units:
            w = jnp.exp(e[uc][lv * CHUNK:(lv + 1) * CHUNK])
            a = lax.dot_general((q[uc] * w).astype(bf16), (k[uc] * w).astype(bf16), NT, preferred_element_type=f32)
            scores[uc] = jnp.where(lvl == lv, a, scores[uc])
    qi, upd, total, o_intra = {}, {}, {}, {}
    for uc in units:
        a = lax.dot_general(q[uc].astype(bf16), k[uc].astype(bf16), NT, preferred_element_type=f32)
        scores[uc] = jnp.where(lvl == DIAG_LEVEL, a, scores[uc])
        cum = e[uc][0:CHUNK]
        qi[uc] = (q[uc] * jnp.exp(cum)).astype(bf16)
        kd = (k[uc] * jnp.exp(e[uc][KDEC_BLOCK * CHUNK:(KDEC_BLOCK + 1) * CHUNK])).astype(bf16)
        upd[uc] = lax.dot_general(v[uc], kd, TN, preferred_element_type=f32)
        total[uc] = jnp.exp(cum[0:1] if reverse else cum[CHUNK - 1:CHUNK])
    for uc in units:
        o_intra[uc] = jnp.dot(scores[uc].astype(bf16), v[uc], preferred_element_type=f32)

    st = [st_ref[u] for u in range(hps)]
    for u, c in units:
        o = lax.dot_general(qi[u, c], st[u].astype(bf16), NT, preferred_element_type=f32) + o_intra[u, c]
        st[u] = st[u] * total[u, c] + upd[u, c]
        if final:
            o = o + ob_ref[rows[c], vcol[u]]
            o = o * lax.rsqrt(jnp.mean(o * o, axis=-1, keepdims=True) + RMS_EPS) * nw_ref[...]
            o = o * jax.nn.silu(r_ref[rows[c], vcol[u]].astype(f32))
        o_ref[rows[c], vcol[u]] = o.astype(o_ref.dtype)
    for u in range(hps):
        st_ref[u] = st[u]


def _gla_direction(proj, lr, gw, gb, bsz, lp, reverse, final_args=None):
    t = proj.shape[0]
    heads = GLA_HEADS
    dk = gw.shape[1] // heads
    dv = 2 * dk
    n_chunks = lp // CHUNK
    cps = _tile(n_chunks, (3, 2, 1))
    rb = cps * CHUNK
    nsteps = n_chunks // cps
    final = final_args is not None
    ms, lvl = _gla_constants(reverse)
    ms = np.concatenate([ms, ms], axis=1)
    nblk = ms.shape[0]
    rank3 = lr.shape[1]

    def rblk(b, s):
        return b * nsteps + ((nsteps - 1 - s) if reverse else s)

    hps = 2 if heads % 2 == 0 else 1
    groups = heads // hps
    k_off = groups
    v_off = groups
    r_off = 2 * groups
    in_specs = [
        pl.BlockSpec((rb, hps * dk), lambda b, h, s: (rblk(b, s), h)),
        pl.BlockSpec((rb, hps * dk), lambda b, h, s: (rblk(b, s), k_off + h)),
        pl.BlockSpec((rb, hps * dv), lambda b, h, s: (rblk(b, s), v_off + h)),
        pl.BlockSpec((rb, rank3), lambda b, h, s: (rblk(b, s), 0)),
        pl.BlockSpec((rank3, hps * dk), lambda b, h, s: (0, h)),
        pl.BlockSpec((1, hps * dk), lambda b, h, s: (0, h)),
        pl.BlockSpec((nblk, 2 * CHUNK), lambda b, h, s: (0, 0)),
        pl.BlockSpec((CHUNK, CHUNK), lambda b, h, s: (0, 0)),
    ]
    ins = [proj, proj, proj, lr, gw, gb.reshape(1, -1), jnp.asarray(ms, bf16), jnp.asarray(lvl)]
    if final:
        o_other, nw = final_args
        in_specs += [
            pl.BlockSpec((rb, hps * dv), lambda b, h, s: (rblk(b, s), r_off + h)),
            pl.BlockSpec((rb, hps * dv), lambda b, h, s: (rblk(b, s), h)),
            pl.BlockSpec((1, dv), lambda b, h, s: (0, 0)),
        ]
        ins += [proj, o_other, nw.reshape(1, dv)]
    return pl.pallas_call(
        functools.partial(_gla_kernel, reverse=reverse, cps=cps, hps=hps, final=final, dk=dk),
        grid=(bsz, groups, nsteps),
        in_specs=in_specs,
        out_specs=pl.BlockSpec((rb, hps * dv), lambda b, h, s: (rblk(b, s), h)),
        out_shape=jax.ShapeDtypeStruct((t, heads * dv), bf16 if final else f32),
        scratch_shapes=[pltpu.VMEM((hps, dv, dk), f32)],
        compiler_params=_params("parallel", "parallel", "arbitrary"),
        name="gla_final" if final else "gla_partial",
    )(*ins)


N_VARIANTS = WIN_H


def _na_bias_tables(rpb, grid_rows):
    del grid_rows
    n_layers, heads = rpb.shape[0], rpb.shape[1]
    cols = np.arange(GRID_W)
    col_start = np.clip(cols - WIN_W // 2, 0, GRID_W - WIN_W)
    kc = np.arange(GRID_W)
    in_win = (kc[None, :] >= col_start[:, None]) & (kc[None, :] < col_start[:, None] + WIN_W)
    dc = np.clip(kc[None, :] - cols[:, None] + (WIN_W - 1), 0, 2 * WIN_W - 2)
    rpb_c = jnp.where(in_win[None, None, None], rpb[:, :, :, dc], NEG)
    tbl = jnp.stack([rpb_c[:, :, vi:vi + WIN_H] for vi in range(N_VARIANTS)], axis=2)
    tbl = tbl.reshape(n_layers, heads // 2, 2, N_VARIANTS, WIN_H, GRID_W, GRID_W)
    tbl = jnp.transpose(tbl, (0, 1, 3, 2, 5, 4, 6))
    return tbl.reshape(n_layers, heads // 2, N_VARIANTS, 2 * GRID_W, WIN_H * GRID_W)


def _na_kernel(q_ref, k_ref, v_ref, bias_ref, mb_ref, o_ref, *, bps, nsteps, grid_rows):
    lane = lax.broadcasted_iota(jnp.int32, (1, 2 * NA_HEAD_DIM), 1)
    head0 = lane < NA_HEAD_DIM
    km = k_ref[PAD:FRONT, :]
    vm = v_ref[PAD:FRONT, :]
    mb = mb_ref[0]
    mbias = jnp.concatenate([jnp.broadcast_to(mb[0:1], (GRID_W, N_META)),
                             jnp.broadcast_to(mb[1:2], (GRID_W, N_META))], axis=0)
    zero = jnp.zeros((), bf16)
    scale = NA_HEAD_DIM ** -0.5

    def stacked_queries(rows):
        q2 = q_ref[rows, :] * jnp.asarray(scale, bf16)
        return jnp.concatenate([jnp.where(head0, q2, zero), jnp.where(head0, zero, q2)], axis=0)

    def unstack(o):
        return jnp.where(head0, o[0:GRID_W], o[GRID_W:2 * GRID_W])

    row = lax.broadcasted_iota(jnp.int32, (GRID_W, 1), 0)
    blocks = range(bps)

    def step_body(step, carry):
        qq, vv, sl, sm, front, rows = {}, {}, {}, {}, {}, {}
        for j in blocks:
            g = step * bps + j
            r = g - 1
            r0 = jnp.clip(r - WIN_H // 2, 0, grid_rows - WIN_H)
            vi = jnp.clip(r0 - r + WIN_H // 2 + 3, 0, N_VARIANTS - 1)
            ks = pl.multiple_of((r0 + 1) * GRID_W, GRID_W)
            kk = k_ref[pl.ds(ks, WIN_H * GRID_W), :]
            vv[j] = v_ref[pl.ds(ks, WIN_H * GRID_W), :]
            rows[j] = pl.ds(pl.multiple_of(g * GRID_W, GRID_W), GRID_W)
            qq[j] = stacked_queries(rows[j])
            front[j] = g == 0
            sl[j] = lax.dot_general(qq[j], kk, NT, preferred_element_type=f32) + bias_ref[0, vi]
            sm[j] = lax.dot_general(qq[j], km, NT, preferred_element_type=f32) + mbias
        p_loc, p_meta, den = {}, {}, {}
        for j in blocks:
            s_loc = sl[j] + jnp.where(front[j], NEG, 0.0) if j == 0 else sl[j]
            mx = jnp.maximum(jnp.max(s_loc, axis=-1, keepdims=True), jnp.max(sm[j], axis=-1, keepdims=True))
            p_loc[j] = jnp.exp(s_loc - mx)
            p_meta[j] = jnp.exp(sm[j] - mx)
            den[j] = jnp.sum(p_loc[j], axis=-1, keepdims=True) + jnp.sum(p_meta[j], axis=-1, keepdims=True)
        for j in blocks:
            o = (jnp.dot(p_loc[j].astype(bf16), vv[j], preferred_element_type=f32)
                 + jnp.dot(p_meta[j].astype(bf16), vm, preferred_element_type=f32))
            o = unstack(o / den[j])
            if j == 0:
                o = jnp.where((row >= PAD) | jnp.logical_not(front[j]), o, 0.0)
            o_ref[rows[j], :] = o.astype(o_ref.dtype)
        return carry

    lax.fori_loop(0, nsteps, step_body, 0)


def _neighborhood_attention(proj, bias_tbl, meta_bias, bsz, lp, q_col):
    t = proj.shape[0]
    n_pairs = bias_tbl.shape[0]
    width = 2 * NA_HEAD_DIM
    grid_rows = (lp - FRONT) // GRID_W
    assert grid_rows >= WIN_H
    n_blocks = lp // GRID_W
    bps = _tile(n_blocks, (3, 2, 1))
    nsteps = n_blocks // bps
    qb = q_col // width
    return pl.pallas_call(
        functools.partial(_na_kernel, bps=bps, nsteps=nsteps, grid_rows=grid_rows),
        grid=(bsz, n_pairs),
        in_specs=[
            pl.BlockSpec((lp, width), lambda b, p: (b, qb + p)),
            pl.BlockSpec((lp, width), lambda b, p: (b, qb + n_pairs + p)),
            pl.BlockSpec((lp, width), lambda b, p: (b, qb + 2 * n_pairs + p)),
            pl.BlockSpec((1,) + bias_tbl.shape[1:], lambda b, p: (p, 0, 0, 0)),
            pl.BlockSpec((1, 2, N_META), lambda b, p: (p, 0, 0)),
        ],
        out_specs=pl.BlockSpec((lp, width), lambda b, p: (b, p)),
        out_shape=jax.ShapeDtypeStruct((t, n_pairs * width), bf16),
        compiler_params=_params("parallel", "parallel"),
        name="neighborhood_attention",
    )(proj, proj, proj, bias_tbl, meta_bias.reshape(n_pairs, 2, N_META))


ISSUE_UNROLL = 8


def _start_row_gather(src_hbm, idx_smem, slot, dst, sem, n, ns):
    base = slot * n

    def body(j, c):
        r0 = j * ISSUE_UNROLL
        for u in range(ISSUE_UNROLL):
            src_row = pl.multiple_of(idx_smem[base + r0 + u] * ns, ns)
            dst_row = pl.multiple_of((r0 + u) * ns, ns)
            pltpu.make_async_copy(src_hbm.at[pl.ds(src_row, ns), :], dst.at[slot, pl.ds(dst_row, ns), :],
                                  sem.at[slot]).start()
        return c

    lax.fori_loop(0, n // ISSUE_UNROLL, body, 0)


def _wait_row_gather(src_hbm, slot, dst, sem, n, ns):
    pltpu.make_async_copy(src_hbm.at[pl.ds(0, n * ns), :], dst.at[slot], sem.at[slot]).wait()


def _gather_pipeline(idx_hbm, src_hbm, idx_smem, buf, sem_idx, sem_rows, n, ns):
    g = pl.program_id(0)
    last = pl.num_programs(0) - 1
    slot = g % 2
    nxt = 1 - slot

    def idx_copy(step, s):
        return pltpu.make_async_copy(idx_hbm.at[step], idx_smem.at[pl.ds(s * n, n)], sem_idx.at[s])

    @pl.when(g == 0)
    def _():
        c = idx_copy(0, 0)
        c.start()
        c.wait()
        _start_row_gather(src_hbm, idx_smem, 0, buf, sem_rows, n, ns)

        @pl.when(last >= 1)
        def _():
            idx_copy(1, 1).start()

    @pl.when(g < last)
    def _():
        idx_copy(g + 1, nxt).wait()
        _start_row_gather(src_hbm, idx_smem, nxt, buf, sem_rows, n, ns)

    @pl.when(g + 1 < last)
    def _():
        idx_copy(g + 2, slot).start()

    _wait_row_gather(src_hbm, slot, buf, sem_rows, n, ns)
    return slot


def _expert_kernel(ge_ref, nu_ref, idx_hbm, x_hbm, w1_ref, b1_ref, w2_ref, b2_ref, rw_ref, y_ref,
                   idx_smem, xbuf, sem_idx, sem_rows):
    del ge_ref
    ns = xbuf.shape[1] // GROUP_ROWS
    slot = _gather_pipeline(idx_hbm, x_hbm, idx_smem, xbuf, sem_idx, sem_rows, GROUP_ROWS, ns)
    g = pl.program_id(0)
    ff = w2_ref.shape[2]

    @pl.when(g < nu_ref[0])
    def _():
        x = jnp.concatenate([_slab_column(xbuf, (slot,), 0, GROUP_ROWS, ns, c) for c in range(ns)], axis=1).astype(bf16)
        h = jnp.dot(x, w1_ref[0, 0], preferred_element_type=f32) + b1_ref[0]
        x_glu = jnp.minimum(h[:, :ff], SWIGLU_LIMIT)
        x_lin = jnp.clip(h[:, ff:], -SWIGLU_LIMIT, SWIGLU_LIMIT)
        act = x_glu * jax.nn.sigmoid(SWIGLU_ALPHA * x_glu) * (x_lin + 1.0)
        y = jnp.dot(act.astype(bf16), w2_ref[0, 0], preferred_element_type=f32) + b2_ref[0]
        _rows_to_slabs(y_ref, y * rw_ref[...])

    @pl.when(g >= nu_ref[0])
    def _():
        y_ref[...] = jnp.zeros_like(y_ref)


def _expert_groups(hs, row_tok, row_w, group_e, n_used, w1, b1, w2, b2, layer):
    _, n_exp, d, f2 = w1.shape
    nslab = d // LANES
    n_groups = row_tok.shape[0]
    ff = f2 // 2
    grid_spec = pltpu.PrefetchScalarGridSpec(
        num_scalar_prefetch=2,
        grid=(n_groups,),
        in_specs=[
            pl.BlockSpec(memory_space=pl.ANY),
            pl.BlockSpec(memory_space=pl.ANY),
            pl.BlockSpec((1, 1, d, f2), lambda g, ge, nu: (layer, ge[g], 0, 0)),
            pl.BlockSpec((1, 1, f2), lambda g, ge, nu: (ge[g], 0, 0)),
            pl.BlockSpec((1, 1, ff, d), lambda g, ge, nu: (layer, ge[g], 0, 0)),
            pl.BlockSpec((1, 1, d), lambda g, ge, nu: (ge[g], 0, 0)),
            pl.BlockSpec((GROUP_ROWS, 1), lambda g, ge, nu: (g, 0)),
        ],
        out_specs=pl.BlockSpec((GROUP_ROWS * nslab, LANES), lambda g, ge, nu: (g, 0)),
        scratch_shapes=[
            pltpu.SMEM((2 * GROUP_ROWS,), jnp.int32),
            pltpu.VMEM((2, GROUP_ROWS * nslab, LANES), f32),
            pltpu.SemaphoreType.DMA((2,)),
            pltpu.SemaphoreType.DMA((2,)),
        ],
    )
    return pl.pallas_call(
        _expert_kernel, grid_spec=grid_spec,
        out_shape=jax.ShapeDtypeStruct((n_groups * GROUP_ROWS * nslab, LANES), f32),
        compiler_params=_params("arbitrary"), name="expert_groups",
    )(group_e, n_used, row_tok, hs, w1, b1.reshape(n_exp, 1, f2), w2, b2.reshape(n_exp, 1, d),
      row_w.reshape(n_groups * GROUP_ROWS, 1))


COMBINE_TOKENS = 128


def _combine_ln_kernel(*refs, alpha, has_proj):
    it = iter(refs)
    idx_hbm, y_hbm, res_ref, w_ref, b_ref = (next(it) for _ in range(5))
    pw_ref, pb_ref = (next(it), next(it)) if has_proj else (None, None)
    h_ref, hb_ref = next(it), next(it)
    p_ref = next(it) if has_proj else None
    idx_smem, buf, sem_idx, sem_rows = (next(it) for _ in range(4))
    tc = COMBINE_TOKENS
    ns = res_ref.shape[0] // tc
    slot = _gather_pipeline(idx_hbm, y_hbm, idx_smem, buf, sem_idx, sem_rows, TOP_K * tc, ns)
    cols = []
    for c in range(ns):
        acc = _slab_column(buf, (slot,), 0, tc, ns, c)
        for kk in range(1, TOP_K):
            acc = acc + _slab_column(buf, (slot,), kk * tc, tc, ns, c)
        cols.append(alpha * _slab_column(res_ref, (), 0, tc, ns, c) + acc)
    h = _ln_math(jnp.concatenate(cols, axis=1), w_ref[...], b_ref[...])
    h_ref[...] = h
    hb_ref[...] = h.astype(bf16)
    if has_proj:
        p_ref[...] = _small_proj(h, pw_ref, pb_ref)


def _combine_layer_norm(y_rows, dest, hs, alpha, w, b, pw=None, pb=None):
    d = w.shape[0]
    nslab = d // LANES
    t = hs.shape[0] // nslab
    tc = COMBINE_TOKENS
    n_tiles = t // tc
    has_proj = pw is not None
    idx = dest.reshape(n_tiles, tc, TOP_K).transpose(0, 2, 1).reshape(n_tiles, TOP_K * tc)
    row = pl.BlockSpec((tc, d), lambda i: (i, 0))
    vec = pl.BlockSpec((1, d), lambda i: (0, 0))
    ins = [idx, y_rows, hs, w.reshape(1, d), b.reshape(1, d)]
    in_specs = [pl.BlockSpec(memory_space=pl.ANY), pl.BlockSpec(memory_space=pl.ANY),
                pl.BlockSpec((tc * nslab, LANES), lambda i: (i, 0)), vec, vec]
    outs = [jax.ShapeDtypeStruct((t, d), f32), jax.ShapeDtypeStruct((t, d), bf16)]
    out_specs = [row, row]
    if has_proj:
        ins += [pw, pb.reshape(1, SMALL_N)]
        in_specs += [pl.BlockSpec((3 * d, SMALL_N), lambda i: (0, 0)), pl.BlockSpec((1, SMALL_N), lambda i: (0, 0))]
        outs.append(jax.ShapeDtypeStruct((t, SMALL_N), f32))
        out_specs.append(pl.BlockSpec((tc, SMALL_N), lambda i: (i, 0)))
    return pl.pallas_call(
        functools.partial(_combine_ln_kernel, alpha=alpha, has_proj=has_proj),
        grid=(n_tiles,), in_specs=in_specs, out_specs=out_specs, out_shape=outs,
        scratch_shapes=[
            pltpu.SMEM((2 * TOP_K * tc,), jnp.int32),
            pltpu.VMEM((2, TOP_K * tc * nslab, LANES), f32),
            pltpu.SemaphoreType.DMA((2,)),
            pltpu.SemaphoreType.DMA((2,)),
        ],
        compiler_params=_params("arbitrary"), name="combine_layer_norm",
    )(*ins)


def _route(top_e, gate, n_exp):
    t = top_e.shape[0]
    n = t * TOP_K
    flat_e = top_e.reshape(-1)
    pos = jnp.arange(n, dtype=jnp.int32)
    e_sorted, order = lax.sort((flat_e, pos), num_keys=1, is_stable=True)
    _, inv = lax.sort((order, pos), num_keys=1)
    bounds = jnp.searchsorted(e_sorted, jnp.arange(n_exp + 1, dtype=jnp.int32), side="left",
                              method="scan_unrolled").astype(jnp.int32)
    start = bounds[:n_exp]
    counts = bounds[1:] - start
    padded = (counts + GROUP_ROWS - 1) // GROUP_ROWS * GROUP_ROWS
    pend = jnp.cumsum(padded)
    pstart = pend - padded
    dest = (inv + (pstart - start)[flat_e]).reshape(t, TOP_K)
    n_groups = -(-n // GROUP_ROWS) + n_exp
    group_row0 = jnp.arange(n_groups, dtype=jnp.int32) * GROUP_ROWS
    group_e = jnp.minimum(jnp.sum(pend[None, :] <= group_row0[:, None], axis=1, dtype=jnp.int32), n_exp - 1)
    within = (group_row0 - pstart[group_e])[:, None] + jnp.arange(GROUP_ROWS, dtype=jnp.int32)[None, :]
    live = (within < counts[group_e][:, None]) & (group_row0 < pend[n_exp - 1])[:, None]
    src = jnp.clip(start[group_e][:, None] + within, 0, n - 1)
    pair = order[src]
    row_tok = jnp.where(live, pair // TOP_K, 0).astype(jnp.int32)
    row_w = jnp.where(live, gate.reshape(-1)[pair], 0.0)
    n_used = (pend[n_exp - 1] // GROUP_ROWS).astype(jnp.int32).reshape(1)
    return row_tok, row_w, group_e, n_used, dest


def kernel(x, meta_tokens, ln0_w, ln0_b, w_in, gk_fwd_w, gk_fwd_b, gk_bwd_w, gk_bwd_b, gla_norm_w, w_branch_a, rpb, meta_bias, w_branch_b, w_out, ln1_w, ln1_b, router_w, router_b, w1, b1, w2, b2, ln2_w, ln2_b):
    bsz, seq, d = x.shape
    depth = w_in.shape[0]
    alpha = (2 * depth) ** 0.25
    lp = FRONT + seq
    kdim = gk_fwd_w.shape[2]
    vdim = w_branch_a.shape[1]
    ndim = w_branch_b.shape[1]
    n_exp = router_w.shape[2]
    ff = w2.shape[2]
    assert seq % GRID_W == 0 and n_exp <= SMALL_N and 2 * GATE_RANK <= SMALL_N

    lr_col = 2 * kdim + 2 * vdim
    nq_col = lr_col
    ga_col = nq_col + 3 * ndim
    gb_col = ga_col + d
    w_main = jnp.concatenate([w_in[:, :, :lr_col], w_in[:, :, lr_col + 2 * GATE_RANK:]], axis=2).astype(bf16)
    w_lr = _split_rhs(jnp.pad(w_in[:, :, lr_col:lr_col + 2 * GATE_RANK], ((0, 0), (0, 0), (0, SMALL_N - 2 * GATE_RANK))))
    w_rt = _split_rhs(jnp.pad(router_w, ((0, 0), (0, 0), (0, SMALL_N - n_exp))))
    b_rt = jnp.pad(router_b, ((0, 0), (0, SMALL_N - n_exp)))
    zeros_small = jnp.zeros((SMALL_N,), f32)
    w_a = w_branch_a.astype(bf16)
    w_b = w_branch_b.astype(bf16)
    w_o = w_out.astype(bf16)
    w1p = _deinterleave_cast(w1.reshape(depth * n_exp, d, 2 * ff)).reshape(depth, n_exp, d, 2 * ff)
    b1p = jnp.concatenate([b1[..., 0::2], b1[..., 1::2]], axis=-1)
    w2b = w2.astype(bf16)
    bias_tbl = _na_bias_tables(rpb, seq // GRID_W)
    gw_f, gw_b = _split_rhs(gk_fwd_w), _split_rhs(gk_bwd_w)

    front = jnp.concatenate([jnp.zeros((PAD, d), x.dtype), meta_tokens.astype(x.dtype)], axis=0)
    xp = jnp.concatenate([jnp.broadcast_to(front[None], (bsz, FRONT, d)), x], axis=1).reshape(bsz * lp, d)
    h, hb, small = _embed_layer_norm(xp, ln0_w, ln0_b, w_lr[0], zeros_small)

    for l in range(depth):
        proj = _matmul(hb, w_main, l, bf16, "in_proj")
        lr_f = _split_lhs(small[:, :GATE_RANK])
        lr_b = _split_lhs(small[:, GATE_RANK:2 * GATE_RANK])
        o_bwd = _gla_direction(proj, lr_b, gw_b[l], gk_bwd_b[l], bsz, lp, reverse=True)
        o_a = _gla_direction(proj, lr_f, gw_f[l], gk_fwd_b[l], bsz, lp, reverse=False,
                             final_args=(o_bwd, gla_norm_w[l]))
        o_b = _neighborhood_attention(proj, bias_tbl[l], meta_bias[l], bsz, lp, nq_col)
        mixed = _branch_merge(o_a, o_b, w_a[l], w_b[l], proj, ga_col, gb_col)
        hs, top_e, gate = _mixer_layer_norm(mixed, w_o[l], h, alpha, ln1_w[l], ln1_b[l], w_rt[l], b_rt[l], n_exp)

        row_tok, row_w, group_e, n_used, dest = _route(top_e[:, :TOP_K], gate[:, :TOP_K], n_exp)
        y_rows = _expert_groups(hs, row_tok, row_w, group_e, n_used, w1p, b1p[l], w2b, b2[l], l)
        if l + 1 < depth:
            h, hb, small = _combine_layer_norm(y_rows, dest, hs, alpha, ln2_w[l], ln2_b[l], w_lr[l + 1], zeros_small)
        else:
            h, hb = _combine_layer_norm(y_rows, dest, hs, alpha, ln2_w[l], ln2_b[l])
    del ff, vdim
    return h.reshape(bsz, lp, d)[:, FRONT:]
```

```python
import functools

import numpy as np
import jax
import jax.numpy as jnp
from jax import lax
from jax.experimental import pallas as pl
from jax.experimental.pallas import tpu as pltpu

N_META = 16
GRID_W = 64
GLA_HEADS = 4
GATE_RANK = 16
GATE_NORMALIZER = 16.0
CHUNK = 64
NA_HEAD_DIM = 64
WIN_H = 8
WIN_W = 16
TOP_K = 4
SWIGLU_ALPHA = 1.702
SWIGLU_LIMIT = 7.0
GROUP_ROWS = 512
LN_EPS = 1e-5
RMS_EPS = 1e-6

FRONT = CHUNK
PAD = FRONT - N_META
SMALL_N = 128
NEG = -1e30
VMEM_LIMIT = 52 * 1024 * 1024
HIGHEST = lax.Precision.HIGHEST
NT = (((1,), (1,)), ((), ()))
TN = (((0,), (0,)), ((), ()))

f32 = jnp.float32
bf16 = jnp.bfloat16


def _params(*sem):
    return pltpu.CompilerParams(dimension_semantics=sem, vmem_limit_bytes=VMEM_LIMIT)


def _tile(n, prefs):
    for p in prefs:
        if n % p == 0:
            return p
    raise ValueError(f"no tile in {prefs} divides {n}")


LANES = 128


def _ln_math(z, w, b):
    mu = jnp.mean(z, axis=-1, keepdims=True)
    d = z - mu
    var = jnp.mean(d * d, axis=-1, keepdims=True)
    return d * lax.rsqrt(var + LN_EPS) * w + b


def _small_proj(h, pw_ref, pb_ref):
    h_hi = h.astype(bf16)
    h_lo = (h - h_hi.astype(f32)).astype(bf16)
    lhs = jnp.concatenate([h_hi, h_hi, h_lo], axis=1)
    return jnp.dot(lhs, pw_ref[...], preferred_element_type=f32) + pb_ref[...]


def _rows_to_slabs(ref, rows):
    n, d = rows.shape
    ns = d // LANES
    for c in range(ns):
        ref[pl.ds(c, n, stride=ns), :] = rows[:, c * LANES:(c + 1) * LANES]


def _slab_column(ref, lead, first_token, n, ns, c):
    return ref[lead + (pl.ds(first_token * ns + c, n, stride=ns), slice(None))]


def _embed_ln_kernel(x_ref, w_ref, b_ref, pw_ref, pb_ref, h_ref, hb_ref, p_ref):
    h = _ln_math(x_ref[...], w_ref[...], b_ref[...])
    h_ref[...] = h
    hb_ref[...] = h.astype(bf16)
    p_ref[...] = _small_proj(h, pw_ref, pb_ref)


def _embed_layer_norm(x, w, b, pw, pb):
    t, d = x.shape
    tm = _tile(t, (256, 128, 64))
    row = pl.BlockSpec((tm, d), lambda i: (i, 0))
    vec = pl.BlockSpec((1, d), lambda i: (0, 0))
    small = pl.BlockSpec((tm, SMALL_N), lambda i: (i, 0))
    return pl.pallas_call(
        _embed_ln_kernel, grid=(t // tm,),
        in_specs=[row, vec, vec, pl.BlockSpec((3 * d, SMALL_N), lambda i: (0, 0)),
                  pl.BlockSpec((1, SMALL_N), lambda i: (0, 0))],
        out_specs=[row, row, small],
        out_shape=[jax.ShapeDtypeStruct((t, d), f32), jax.ShapeDtypeStruct((t, d), bf16),
                   jax.ShapeDtypeStruct((t, SMALL_N), f32)],
        compiler_params=_params("parallel"), name="embed_layer_norm",
    )(x, w.reshape(1, d), b.reshape(1, d), pw, pb.reshape(1, SMALL_N))


def _mixer_ln_kernel(x_ref, mw_ref, res_ref, w_ref, b_ref, pw_ref, pb_ref, hs_ref, te_ref, tg_ref, *, alpha, n_exp):
    z = alpha * res_ref[...] + jnp.dot(x_ref[...], mw_ref[...], preferred_element_type=f32)
    h = _ln_math(z, w_ref[...], b_ref[...])
    _rows_to_slabs(hs_ref, h)
    lane = lax.broadcasted_iota(jnp.int32, (1, SMALL_N), 1)
    logits = jnp.where(lane < n_exp, _small_proj(h, pw_ref, pb_ref), NEG)
    top_e = jnp.zeros(logits.shape, jnp.int32)
    vals = []
    for kk in range(TOP_K):
        m = jnp.max(logits, axis=-1, keepdims=True)
        idx = jnp.min(jnp.where(logits == m, lane, SMALL_N), axis=-1, keepdims=True)
        top_e = jnp.where(lane == kk, idx, top_e)
        vals.append(m)
        logits = jnp.where(lane == idx, NEG, logits)
    ex = [jnp.exp(v - vals[0]) for v in vals]
    den = ex[0]
    for e_k in ex[1:]:
        den = den + e_k
    gate = jnp.zeros(logits.shape, f32)
    for kk in range(TOP_K):
        gate = jnp.where(lane == kk, ex[kk] / den, gate)
    te_ref[...] = top_e
    tg_ref[...] = gate


def _mixer_layer_norm(x, mm_w, res, alpha, w, b, pw, pb, n_exp):
    t, k = x.shape
    d = w.shape[0]
    ns = d // LANES
    tm = _tile(t, (256, 128, 64))
    row = pl.BlockSpec((tm, d), lambda i: (i, 0))
    vec = pl.BlockSpec((1, d), lambda i: (0, 0))
    small = pl.BlockSpec((tm, SMALL_N), lambda i: (i, 0))
    return pl.pallas_call(
        functools.partial(_mixer_ln_kernel, alpha=alpha, n_exp=n_exp), grid=(t // tm,),
        in_specs=[pl.BlockSpec((tm, k), lambda i: (i, 0)), pl.BlockSpec((k, d), lambda i: (0, 0)), row, vec, vec,
                  pl.BlockSpec((3 * d, SMALL_N), lambda i: (0, 0)), pl.BlockSpec((1, SMALL_N), lambda i: (0, 0))],
        out_specs=[pl.BlockSpec((tm * ns, LANES), lambda i: (i, 0)), small, small],
        out_shape=[jax.ShapeDtypeStruct((t * ns, LANES), f32), jax.ShapeDtypeStruct((t, SMALL_N), jnp.int32),
                   jax.ShapeDtypeStruct((t, SMALL_N), f32)],
        compiler_params=_params("parallel"), name="mixer_layer_norm",
    )(x, mm_w, res, w.reshape(1, d), b.reshape(1, d), pw, pb.reshape(1, SMALL_N))


def _mm_kernel(x_ref, w_ref, o_ref):
    o_ref[...] = jnp.dot(x_ref[...], w_ref[0], preferred_element_type=f32).astype(o_ref.dtype)


def _matmul(x, w, layer, out_dtype, name):
    m, k = x.shape
    n = w.shape[2]
    tm = _tile(m, (768, 512, 384, 256, 128))
    tn = _tile(n, (1024, 512, 256, 128))
    return pl.pallas_call(
        _mm_kernel,
        grid=(n // tn, m // tm),
        in_specs=[pl.BlockSpec((tm, k), lambda j, i: (i, 0)), pl.BlockSpec((1, k, tn), lambda j, i: (layer, 0, j))],
        out_specs=pl.BlockSpec((tm, tn), lambda j, i: (i, j)),
        out_shape=jax.ShapeDtypeStruct((m, n), out_dtype),
        compiler_params=_params("parallel", "parallel"), name=name,
    )(x, w)


LANE_PAIR = 256


def _deinterleave_kernel(x_ref, p_ref, o_ref):
    half = o_ref.shape[-1] // 2
    out_w = LANE_PAIR // 2
    for j in range(x_ref.shape[-1] // LANE_PAIR):
        xb = x_ref[0, :, j * LANE_PAIR:(j + 1) * LANE_PAIR].astype(bf16)
        y = jnp.dot(xb, p_ref[...], preferred_element_type=f32).astype(bf16)
        o_ref[0, :, j * out_w:(j + 1) * out_w] = y[:, :out_w]
        o_ref[0, :, half + j * out_w:half + (j + 1) * out_w] = y[:, out_w:]


def _deinterleave_cast(w):
    n, d, f2 = w.shape
    assert f2 % LANE_PAIR == 0
    tr = _tile(d, (1024, 512, 256, 128))
    perm = np.zeros((LANE_PAIR, LANE_PAIR), np.float32)
    idx = np.arange(LANE_PAIR // 2)
    perm[2 * idx, idx] = 1.0
    perm[2 * idx + 1, LANE_PAIR // 2 + idx] = 1.0
    return pl.pallas_call(
        _deinterleave_kernel,
        grid=(n, d // tr),
        in_specs=[pl.BlockSpec((1, tr, f2), lambda i, r: (i, r, 0)),
                  pl.BlockSpec((LANE_PAIR, LANE_PAIR), lambda i, r: (0, 0))],
        out_specs=pl.BlockSpec((1, tr, f2), lambda i, r: (i, r, 0)),
        out_shape=jax.ShapeDtypeStruct((n, d, f2), bf16),
        compiler_params=_params("parallel", "parallel"), name="deinterleave_cast",
    )(w, jnp.asarray(perm, bf16))


def _split_lhs(a):
    hi = a.astype(bf16)
    lo = (a - hi.astype(f32)).astype(bf16)
    return jnp.concatenate([hi, hi, lo], axis=1)


def _split_rhs(w):
    hi = w.astype(bf16)
    lo = (w - hi.astype(f32)).astype(bf16)
    return jnp.concatenate([hi, lo, hi], axis=-2)


def _branch_kernel(oa_ref, ob_ref, wa_ref, wb_ref, ga_ref, gb_ref, o_ref):
    ya = jnp.dot(oa_ref[...], wa_ref[...], preferred_element_type=f32)
    yb = jnp.dot(ob_ref[...], wb_ref[...], preferred_element_type=f32)
    mixed = jax.nn.sigmoid(ga_ref[...].astype(f32)) * ya + jax.nn.sigmoid(gb_ref[...].astype(f32)) * yb
    o_ref[...] = mixed.astype(o_ref.dtype)


def _branch_merge(o_a, o_b, w_a, w_b, proj, ga_col, gb_col):
    t, ka = o_a.shape
    kb = o_b.shape[1]
    d = w_a.shape[1]
    tm = _tile(t, (768, 512, 384, 256, 128))
    tn = 512
    assert ga_col % tn == 0 and gb_col % tn == 0 and d % tn == 0
    return pl.pallas_call(
        _branch_kernel,
        grid=(d // tn, t // tm),
        in_specs=[
            pl.BlockSpec((tm, ka), lambda j, i: (i, 0)),
            pl.BlockSpec((tm, kb), lambda j, i: (i, 0)),
            pl.BlockSpec((ka, tn), lambda j, i: (0, j)),
            pl.BlockSpec((kb, tn), lambda j, i: (0, j)),
            pl.BlockSpec((tm, tn), lambda j, i: (i, ga_col // tn + j)),
            pl.BlockSpec((tm, tn), lambda j, i: (i, gb_col // tn + j)),
        ],
        out_specs=pl.BlockSpec((tm, tn), lambda j, i: (i, j)),
        out_shape=jax.ShapeDtypeStruct((t, d), bf16),
        compiler_params=_params("parallel", "parallel"), name="branch_merge",
    )(o_a, o_b, w_a, w_b, proj, proj)


N_LEVELS = 6
DIAG_LEVEL = N_LEVELS + 1
KDEC_BLOCK = N_LEVELS + 1


def _gla_constants(reverse):
    c = CHUNK
    i = np.arange(c)[:, None]
    s = np.arange(c)[None, :]
    blocks = [(s >= i) if reverse else (s <= i)]
    level = np.zeros((c, c), np.int32)
    size = c
    for lv in range(1, N_LEVELS + 1):
        half = size // 2
        m = np.zeros((c, c), bool)
        for r in range(c):
            base = (r // size) * size
            first_half = (r % size) < half
            if not reverse:
                mid = base + half - 1
                if first_half:
                    m[r, r + 1:mid + 1] = True
                else:
                    m[r, mid + 1:r + 1] = True
            else:
                mid = base + half
                if first_half:
                    m[r, r:mid] = True
                else:
                    m[r, mid:r] = True
        blocks.append(m)
        same = (i // size) == (s // size)
        if not reverse:
            cond = same & ((i % size) >= half) & ((s % size) < half)
        else:
            cond = same & ((i % size) < half) & ((s % size) >= half)
        level[cond] = lv
        size = half
    level[np.eye(c, dtype=bool)] = DIAG_LEVEL
    blocks.append((s < i) if reverse else (s > i))
    return np.concatenate(blocks, axis=0).astype(np.float32), level


def _gla_kernel(*refs, reverse, cps, hps, final, dk):
    it = iter(refs)
    q_ref, k_ref, v_ref, lr_ref, gw_ref, gb_ref, ms_ref, lvl_ref = (next(it) for _ in range(8))
    if final:
        r_ref, ob_ref, nw_ref = next(it), next(it), next(it)
    o_ref = next(it)
    st_ref = next(it)
    dv = st_ref.shape[1]
    step = pl.program_id(2)
    nsteps = pl.num_programs(2)

    @pl.when(step == 0)
    def _():
        st_ref[...] = jnp.zeros_like(st_ref)

    blk = (nsteps - 1 - step) if reverse else step
    ms = ms_ref[...]
    lvl = lvl_ref[...]
    row = lax.broadcasted_iota(jnp.int32, (CHUNK, 1), 0)
    scale = dk ** -0.5
    order = list(range(cps))[::-1] if reverse else list(range(cps))
    rows = [slice(c * CHUNK, (c + 1) * CHUNK) for c in range(cps)]
    kcol = [slice(u * dk, (u + 1) * dk) for u in range(hps)]
    vcol = [slice(u * dv, (u + 1) * dv) for u in range(hps)]
    units = [(u, c) for c in order for u in range(hps)]

    q, k, v, e = {}, {}, {}, {}
    for c in order:
        valid = (row >= PAD) | (blk * cps + c > 0)
        x_all = jnp.dot(lr_ref[rows[c], :], gw_ref[...], preferred_element_type=f32) + gb_ref[...]
        for u in range(hps):
            q[u, c] = jnp.where(valid, q_ref[rows[c], kcol[u]].astype(f32) * scale, 0.0)
            k[u, c] = jnp.where(valid, k_ref[rows[c], kcol[u]].astype(f32), 0.0)
            v[u, c] = jnp.where(valid, v_ref[rows[c], vcol[u]], jnp.zeros((), bf16))
            x = x_all[:, kcol[u]]
            g = (jnp.minimum(x, 0.0) - jnp.log1p(jnp.exp(-jnp.abs(x)))) * (1.0 / GATE_NORMALIZER)
            g = jnp.where(valid, g, 0.0)
            g_hi = g.astype(bf16)
            g_lo = (g - g_hi.astype(f32)).astype(bf16)
            e[u, c] = jnp.dot(ms, jnp.concatenate([g_hi, g_lo], axis=0), preferred_element_type=f32)
    scores = {uc: jnp.zeros((CHUNK, CHUNK), f32) for uc in units}
    for lv in range(1, N_LEVELS + 1):
        for uc in units:
            w = jnp.exp(e[uc][lv * CHUNK:(lv + 1) * CHUNK])
            a = lax.dot_general((q[uc] * w).astype(bf16), (k[uc] * w).astype(bf16), NT, preferred_element_type=f32)
            scores[uc] = jnp.where(lvl == lv, a, scores[uc])
    qi, upd, total, o_intra = {}, {}, {}, {}
    for uc in units:
        a = lax.dot_general(q[uc].astype(bf16), k[uc].astype(bf16), NT, preferred_element_type=f32)
        scores[uc] = jnp.where(lvl == DIAG_LEVEL, a, scores[uc])
        cum = e[uc][0:CHUNK]
        qi[uc] = (q[uc] * jnp.exp(cum)).astype(bf16)
        kd = (k[uc] * jnp.exp(e[uc][KDEC_BLOCK * CHUNK:(KDEC_BLOCK + 1) * CHUNK])).astype(bf16)
        upd[uc] = lax.dot_general(v[uc], kd, TN, preferred_element_type=f32)
        total[uc] = jnp.exp(cum[0:1] if reverse else cum[CHUNK - 1:CHUNK])
    for uc in units:
        o_intra[uc] = jnp.dot(scores[uc].astype(bf16), v[uc], preferred_element_type=f32)

    st = [st_ref[u] for u in range(hps)]
    for u, c in units:
        o = lax.dot_general(qi[u, c], st[u].astype(bf16), NT, preferred_element_type=f32) + o_intra[u, c]
        st[u] = st[u] * total[u, c] + upd[u, c]
        if final:
            o = o + ob_ref[rows[c], vcol[u]]
            o = o * lax.rsqrt(jnp.mean(o * o, axis=-1, keepdims=True) + RMS_EPS) * nw_ref[...]
            o = o * jax.nn.silu(r_ref[rows[c], vcol[u]].astype(f32))
        o_ref[rows[c], vcol[u]] = o.astype(o_ref.dtype)
    for u in range(hps):
        st_ref[u] = st[u]


def _gla_direction(proj, lr, gw, gb, bsz, lp, reverse, final_args=None):
    t = proj.shape[0]
    heads = GLA_HEADS
    dk = gw.shape[1] // heads
    dv = 2 * dk
    n_chunks = lp // CHUNK
    cps = _tile(n_chunks, (3, 2, 1))
    rb = cps * CHUNK
    nsteps = n_chunks // cps
    final = final_args is not None
    ms, lvl = _gla_constants(reverse)
    ms = np.concatenate([ms, ms], axis=1)
    nblk = ms.shape[0]
    rank3 = lr.shape[1]

    def rblk(b, s):
        return b * nsteps + ((nsteps - 1 - s) if reverse else s)

    hps = 4 if heads % 4 == 0 else 1
    groups = heads // hps
    k_off = groups
    v_off = groups
    r_off = 2 * groups
    in_specs = [
        pl.BlockSpec((rb, hps * dk), lambda b, h, s: (rblk(b, s), h)),
        pl.BlockSpec((rb, hps * dk), lambda b, h, s: (rblk(b, s), k_off + h)),
        pl.BlockSpec((rb, hps * dv), lambda b, h, s: (rblk(b, s), v_off + h)),
        pl.BlockSpec((rb, rank3), lambda b, h, s: (rblk(b, s), 0)),
        pl.BlockSpec((rank3, hps * dk), lambda b, h, s: (0, h)),
        pl.BlockSpec((1, hps * dk), lambda b, h, s: (0, h)),
        pl.BlockSpec((nblk, 2 * CHUNK), lambda b, h, s: (0, 0)),
        pl.BlockSpec((CHUNK, CHUNK), lambda b, h, s: (0, 0)),
    ]
    ins = [proj, proj, proj, lr, gw, gb.reshape(1, -1), jnp.asarray(ms, bf16), jnp.asarray(lvl)]
    if final:
        o_other, nw = final_args
        in_specs += [
            pl.BlockSpec((rb, hps * dv), lambda b, h, s: (rblk(b, s), r_off + h)),
            pl.BlockSpec((rb, hps * dv), lambda b, h, s: (rblk(b, s), h)),
            pl.BlockSpec((1, dv), lambda b, h, s: (0, 0)),
        ]
        ins += [proj, o_other, nw.reshape(1, dv)]
    return pl.pallas_call(
        functools.partial(_gla_kernel, reverse=reverse, cps=cps, hps=hps, final=final, dk=dk),
        grid=(bsz, groups, nsteps),
        in_specs=in_specs,
        out_specs=pl.BlockSpec((rb, hps * dv), lambda b, h, s: (rblk(b, s), h)),
        out_shape=jax.ShapeDtypeStruct((t, heads * dv), bf16 if final else f32),
        scratch_shapes=[pltpu.VMEM((hps, dv, dk), f32)],
        compiler_params=_params("parallel", "parallel", "arbitrary"),
        name="gla_final" if final else "gla_partial",
    )(*ins)


N_VARIANTS = WIN_H


def _na_bias_tables(rpb, grid_rows):
    del grid_rows
    n_layers, heads = rpb.shape[0], rpb.shape[1]
    cols = np.arange(GRID_W)
    col_start = np.clip(cols - WIN_W // 2, 0, GRID_W - WIN_W)
    kc = np.arange(GRID_W)
    in_win = (kc[None, :] >= col_start[:, None]) & (kc[None, :] < col_start[:, None] + WIN_W)
    dc = np.clip(kc[None, :] - cols[:, None] + (WIN_W - 1), 0, 2 * WIN_W - 2)
    rpb_c = jnp.where(in_win[None, None, None], rpb[:, :, :, dc], NEG)
    tbl = jnp.stack([rpb_c[:, :, vi:vi + WIN_H] for vi in range(N_VARIANTS)], axis=2)
    tbl = tbl.reshape(n_layers, heads // 2, 2, N_VARIANTS, WIN_H, GRID_W, GRID_W)
    tbl = jnp.transpose(tbl, (0, 1, 3, 2, 5, 4, 6))
    return tbl.reshape(n_layers, heads // 2, N_VARIANTS, 2 * GRID_W, WIN_H * GRID_W)


def _na_kernel(q_ref, k_ref, v_ref, bias_ref, mb_ref, o_ref, *, bps, nsteps, grid_rows):
    lane = lax.broadcasted_iota(jnp.int32, (1, 2 * NA_HEAD_DIM), 1)
    head0 = lane < NA_HEAD_DIM
    km = k_ref[PAD:FRONT, :]
    vm = v_ref[PAD:FRONT, :]
    mb = mb_ref[0]
    mbias = jnp.concatenate([jnp.broadcast_to(mb[0:1], (GRID_W, N_META)),
                             jnp.broadcast_to(mb[1:2], (GRID_W, N_META))], axis=0)
    zero = jnp.zeros((), bf16)
    scale = NA_HEAD_DIM ** -0.5

    def stacked_queries(rows):
        q2 = q_ref[rows, :] * jnp.asarray(scale, bf16)
        return jnp.concatenate([jnp.where(head0, q2, zero), jnp.where(head0, zero, q2)], axis=0)

    def unstack(o):
        return jnp.where(head0, o[0:GRID_W], o[GRID_W:2 * GRID_W])

    row = lax.broadcasted_iota(jnp.int32, (GRID_W, 1), 0)
    blocks = range(bps)

    def step_body(step, carry):
        qq, vv, sl, sm, front, rows = {}, {}, {}, {}, {}, {}
        for j in blocks:
            g = step * bps + j
            r = g - 1
            r0 = jnp.clip(r - WIN_H // 2, 0, grid_rows - WIN_H)
            vi = jnp.clip(r0 - r + WIN_H // 2 + 3, 0, N_VARIANTS - 1)
            ks = pl.multiple_of((r0 + 1) * GRID_W, GRID_W)
            kk = k_ref[pl.ds(ks, WIN_H * GRID_W), :]
            vv[j] = v_ref[pl.ds(ks, WIN_H * GRID_W), :]
            rows[j] = pl.ds(pl.multiple_of(g * GRID_W, GRID_W), GRID_W)
            qq[j] = stacked_queries(rows[j])
            front[j] = g == 0
            sl[j] = lax.dot_general(qq[j], kk, NT, preferred_element_type=f32) + bias_ref[0, vi]
            sm[j] = lax.dot_general(qq[j], km, NT, preferred_element_type=f32) + mbias
        p_loc, p_meta, den = {}, {}, {}
        for j in blocks:
            s_loc = sl[j] + jnp.where(front[j], NEG, 0.0) if j == 0 else sl[j]
            mx = jnp.maximum(jnp.max(s_loc, axis=-1, keepdims=True), jnp.max(sm[j], axis=-1, keepdims=True))
            p_loc[j] = jnp.exp(s_loc - mx)
            p_meta[j] = jnp.exp(sm[j] - mx)
            den[j] = jnp.sum(p_loc[j], axis=-1, keepdims=True) + jnp.sum(p_meta[j], axis=-1, keepdims=True)
        for j in blocks:
            o = (jnp.dot(p_loc[j].astype(bf16), vv[j], preferred_element_type=f32)
                 + jnp.dot(p_meta[j].astype(bf16), vm, preferred_element_type=f32))
            o = unstack(o / den[j])
            if j == 0:
                o = jnp.where((row >= PAD) | jnp.logical_not(front[j]), o, 0.0)
            o_ref[rows[j], :] = o.astype(o_ref.dtype)
        return carry

    lax.fori_loop(0, nsteps, step_body, 0)


def _neighborhood_attention(proj, bias_tbl, meta_bias, bsz, lp, q_col):
    t = proj.shape[0]
    n_pairs = bias_tbl.shape[0]
    width = 2 * NA_HEAD_DIM
    grid_rows = (lp - FRONT) // GRID_W
    assert grid_rows >= WIN_H
    n_blocks = lp // GRID_W
    bps = _tile(n_blocks, (3, 2, 1))
    nsteps = n_blocks // bps
    qb = q_col // width
    return pl.pallas_call(
        functools.partial(_na_kernel, bps=bps, nsteps=nsteps, grid_rows=grid_rows),
        grid=(bsz, n_pairs),
        in_specs=[
            pl.BlockSpec((lp, width), lambda b, p: (b, qb + p)),
            pl.BlockSpec((lp, width), lambda b, p: (b, qb + n_pairs + p)),
            pl.BlockSpec((lp, width), lambda b, p: (b, qb + 2 * n_pairs + p)),
            pl.BlockSpec((1,) + bias_tbl.shape[1:], lambda b, p: (p, 0, 0, 0)),
            pl.BlockSpec((1, 2, N_META), lambda b, p: (p, 0, 0)),
        ],
        out_specs=pl.BlockSpec((lp, width), lambda b, p: (b, p)),
        out_shape=jax.ShapeDtypeStruct((t, n_pairs * width), bf16),
        compiler_params=_params("parallel", "parallel"),
        name="neighborhood_attention",
    )(proj, proj, proj, bias_tbl, meta_bias.reshape(n_pairs, 2, N_META))


ISSUE_UNROLL = 8


def _start_row_gather(src_hbm, idx_smem, slot, dst, sem, n, ns):
    base = slot * n

    def body(j, c):
        r0 = j * ISSUE_UNROLL
        for u in range(ISSUE_UNROLL):
            src_row = pl.multiple_of(idx_smem[base + r0 + u] * ns, ns)
            dst_row = pl.multiple_of((r0 + u) * ns, ns)
            pltpu.make_async_copy(src_hbm.at[pl.ds(src_row, ns), :], dst.at[slot, pl.ds(dst_row, ns), :],
                                  sem.at[slot]).start()
        return c

    lax.fori_loop(0, n // ISSUE_UNROLL, body, 0)


def _wait_row_gather(src_hbm, slot, dst, sem, n, ns):
    pltpu.make_async_copy(src_hbm.at[pl.ds(0, n * ns), :], dst.at[slot], sem.at[slot]).wait()


def _gather_pipeline(idx_hbm, src_hbm, idx_smem, buf, sem_idx, sem_rows, n, ns):
    g = pl.program_id(0)
    last = pl.num_programs(0) - 1
    slot = g % 2
    nxt = 1 - slot

    def idx_copy(step, s):
        return pltpu.make_async_copy(idx_hbm.at[step], idx_smem.at[pl.ds(s * n, n)], sem_idx.at[s])

    @pl.when(g == 0)
    def _():
        c = idx_copy(0, 0)
        c.start()
        c.wait()
        _start_row_gather(src_hbm, idx_smem, 0, buf, sem_rows, n, ns)

        @pl.when(last >= 1)
        def _():
            idx_copy(1, 1).start()

    @pl.when(g < last)
    def _():
        idx_copy(g + 1, nxt).wait()
        _start_row_gather(src_hbm, idx_smem, nxt, buf, sem_rows, n, ns)

    @pl.when(g + 1 < last)
    def _():
        idx_copy(g + 2, slot).start()

    _wait_row_gather(src_hbm, slot, buf, sem_rows, n, ns)
    return slot


def _expert_kernel(ge_ref, nu_ref, idx_hbm, x_hbm, w1_ref, b1_ref, w2_ref, b2_ref, rw_ref, y_ref,
                   idx_smem, xbuf, sem_idx, sem_rows):
    del ge_ref
    n = GROUP_ROWS
    ns = xbuf.shape[1] // n
    g = pl.program_id(0)
    n_groups = pl.num_programs(0)
    last = n_groups - 1
    slot = g % 2
    nxt = 1 - slot
    ff = w2_ref.shape[2]

    def idx_copy(group, s):
        return pltpu.make_async_copy(idx_hbm.at[group], idx_smem.at[pl.ds(s * n, n)], sem_idx.at[s])

    def wrap(group):
        return jnp.where(group >= n_groups, group - n_groups, group)

    @pl.when(g == 0)
    def _():
        c = idx_copy(0, 0)
        c.start()
        c.wait()
        _start_row_gather(x_hbm, idx_smem, 0, xbuf, sem_rows, n, ns)
        idx_copy(wrap(1), 1).start()

    @pl.when(g < nu_ref[0])
    def _():
        _wait_row_gather(x_hbm, slot, xbuf, sem_rows, n, ns)
        idx_copy(wrap(g + 2), slot).start()
        x = jnp.concatenate([_slab_column(xbuf, (slot,), 0, n, ns, c) for c in range(ns)], axis=1).astype(bf16)
        h = jnp.dot(x, w1_ref[0, 0], preferred_element_type=f32) + b1_ref[0]
        idx_copy(0, nxt).wait()
        for i in range(n):
            src_row = pl.multiple_of(idx_smem[nxt * n + i] * ns, ns)
            pltpu.make_async_copy(x_hbm.at[pl.ds(src_row, ns), :], xbuf.at[nxt, pl.ds(i * ns, ns), :],
                                  sem_rows.at[nxt]).start()
        x_glu = jnp.minimum(h[:, :ff], SWIGLU_LIMIT)
        x_lin = jnp.clip(h[:, ff:], -SWIGLU_LIMIT, SWIGLU_LIMIT)
        act = x_glu * jax.nn.sigmoid(SWIGLU_ALPHA * x_glu) * (x_lin + 1.0)
        y = jnp.dot(act.astype(bf16), w2_ref[0, 0], preferred_element_type=f32) + b2_ref[0]
        _rows_to_slabs(y_ref, y * rw_ref[...])

    @pl.when(g >= nu_ref[0])
    def _():
        _wait_row_gather(x_hbm, slot, xbuf, sem_rows, n, ns)
        idx_copy(wrap(g + 2), slot).start()
        idx_copy(0, nxt).wait()
        _start_row_gather(x_hbm, idx_smem, nxt, xbuf, sem_rows, n, ns)
        y_ref[...] = jnp.zeros_like(y_ref)

    @pl.when(g == last)
    def _():
        _wait_row_gather(x_hbm, nxt, xbuf, sem_rows, n, ns)
        idx_copy(0, slot).wait()


def _expert_groups(hs, row_tok, row_w, group_e, n_used, w1, b1, w2, b2, layer):
    _, n_exp, d, f2 = w1.shape
    nslab = d // LANES
    n_groups = row_tok.shape[0]
    assert n_groups >= 2
    ff = f2 // 2
    grid_spec = pltpu.PrefetchScalarGridSpec(
        num_scalar_prefetch=2,
        grid=(n_groups,),
        in_specs=[
            pl.BlockSpec(memory_space=pl.ANY),
            pl.BlockSpec(memory_space=pl.ANY),
            pl.BlockSpec((1, 1, d, f2), lambda g, ge, nu: (layer, ge[g], 0, 0)),
            pl.BlockSpec((1, 1, f2), lambda g, ge, nu: (ge[g], 0, 0)),
            pl.BlockSpec((1, 1, ff, d), lambda g, ge, nu: (layer, ge[g], 0, 0)),
            pl.BlockSpec((1, 1, d), lambda g, ge, nu: (ge[g], 0, 0)),
            pl.BlockSpec((GROUP_ROWS, 1), lambda g, ge, nu: (g, 0)),
        ],
        out_specs=pl.BlockSpec((GROUP_ROWS * nslab, LANES), lambda g, ge, nu: (g, 0)),
        scratch_shapes=[
            pltpu.SMEM((2 * GROUP_ROWS,), jnp.int32),
            pltpu.VMEM((2, GROUP_ROWS * nslab, LANES), f32),
            pltpu.SemaphoreType.DMA((2,)),
            pltpu.SemaphoreType.DMA((2,)),
        ],
    )
    return pl.pallas_call(
        _expert_kernel, grid_spec=grid_spec,
        out_shape=jax.ShapeDtypeStruct((n_groups * GROUP_ROWS * nslab, LANES), f32),
        compiler_params=_params("arbitrary"), name="expert_groups",
    )(group_e, n_used, row_tok, hs, w1, b1.reshape(n_exp, 1, f2), w2, b2.reshape(n_exp, 1, d),
      row_w.reshape(n_groups * GROUP_ROWS, 1))


COMBINE_TOKENS = 128


def _combine_ln_kernel(*refs, alpha, has_proj):
    it = iter(refs)
    idx_hbm, y_hbm, res_ref, w_ref, b_ref = (next(it) for _ in range(5))
    pw_ref, pb_ref = (next(it), next(it)) if has_proj else (None, None)
    h_ref, hb_ref = next(it), next(it)
    p_ref = next(it) if has_proj else None
    idx_smem, buf, sem_idx, sem_rows = (next(it) for _ in range(4))
    tc = COMBINE_TOKENS
    ns = res_ref.shape[0] // tc
    slot = _gather_pipeline(idx_hbm, y_hbm, idx_smem, buf, sem_idx, sem_rows, TOP_K * tc, ns)
    cols = []
    for c in range(ns):
        acc = _slab_column(buf, (slot,), 0, tc, ns, c)
        for kk in range(1, TOP_K):
            acc = acc + _slab_column(buf, (slot,), kk * tc, tc, ns, c)
        cols.append(alpha * _slab_column(res_ref, (), 0, tc, ns, c) + acc)
    h = _ln_math(jnp.concatenate(cols, axis=1), w_ref[...], b_ref[...])
    h_ref[...] = h
    hb_ref[...] = h.astype(bf16)
    if has_proj:
        p_ref[...] = _small_proj(h, pw_ref, pb_ref)


def _combine_layer_norm(y_rows, dest, hs, alpha, w, b, pw=None, pb=None):
    d = w.shape[0]
    nslab = d // LANES
    t = hs.shape[0] // nslab
    tc = COMBINE_TOKENS
    n_tiles = t // tc
    has_proj = pw is not None
    idx = dest.reshape(n_tiles, tc, TOP_K).transpose(0, 2, 1).reshape(n_tiles, TOP_K * tc)
    row = pl.BlockSpec((tc, d), lambda i: (i, 0))
    vec = pl.BlockSpec((1, d), lambda i: (0, 0))
    ins = [idx, y_rows, hs, w.reshape(1, d), b.reshape(1, d)]
    in_specs = [pl.BlockSpec(memory_space=pl.ANY), pl.BlockSpec(memory_space=pl.ANY),
                pl.BlockSpec((tc * nslab, LANES), lambda i: (i, 0)), vec, vec]
    outs = [jax.ShapeDtypeStruct((t, d), f32), jax.ShapeDtypeStruct((t, d), bf16)]
    out_specs = [row, row]
    if has_proj:
        ins += [pw, pb.reshape(1, SMALL_N)]
        in_specs += [pl.BlockSpec((3 * d, SMALL_N), lambda i: (0, 0)), pl.BlockSpec((1, SMALL_N), lambda i: (0, 0))]
        outs.append(jax.ShapeDtypeStruct((t, SMALL_N), f32))
        out_specs.append(pl.BlockSpec((tc, SMALL_N), lambda i: (i, 0)))
    return pl.pallas_call(
        functools.partial(_combine_ln_kernel, alpha=alpha, has_proj=has_proj),
        grid=(n_tiles,), in_specs=in_specs, out_specs=out_specs, out_shape=outs,
        scratch_shapes=[
            pltpu.SMEM((2 * TOP_K * tc,), jnp.int32),
            pltpu.VMEM((2, TOP_K * tc * nslab, LANES), f32),
            pltpu.SemaphoreType.DMA((2,)),
            pltpu.SemaphoreType.DMA((2,)),
        ],
        compiler_params=_params("arbitrary"), name="combine_layer_norm",
    )(*ins)


def _route(top_e, gate, n_exp):
    t = top_e.shape[0]
    n = t * TOP_K
    flat_e = top_e.reshape(-1)
    pos = jnp.arange(n, dtype=jnp.int32)
    e_sorted, order = lax.sort((flat_e, pos), num_keys=1, is_stable=True)
    _, inv = lax.sort((order, pos), num_keys=1)
    bounds = jnp.searchsorted(e_sorted, jnp.arange(n_exp + 1, dtype=jnp.int32), side="left",
                              method="scan_unrolled").astype(jnp.int32)
    start = bounds[:n_exp]
    counts = bounds[1:] - start
    padded = (counts + GROUP_ROWS - 1) // GROUP_ROWS * GROUP_ROWS
    pend = jnp.cumsum(padded)
    pstart = pend - padded
    dest = (inv + (pstart - start)[flat_e]).reshape(t, TOP_K)
    n_groups = -(-n // GROUP_ROWS) + n_exp
    group_row0 = jnp.arange(n_groups, dtype=jnp.int32) * GROUP_ROWS
    group_e = jnp.minimum(jnp.sum(pend[None, :] <= group_row0[:, None], axis=1, dtype=jnp.int32), n_exp - 1)
    within = (group_row0 - pstart[group_e])[:, None] + jnp.arange(GROUP_ROWS, dtype=jnp.int32)[None, :]
    live = (within < counts[group_e][:, None]) & (group_row0 < pend[n_exp - 1])[:, None]
    src = jnp.clip(start[group_e][:, None] + within, 0, n - 1)
    pair = order[src]
    row_tok = jnp.where(live, pair // TOP_K, 0).astype(jnp.int32)
    row_w = jnp.where(live, gate.reshape(-1)[pair], 0.0)
    n_used = (pend[n_exp - 1] // GROUP_ROWS).astype(jnp.int32).reshape(1)
    return row_tok, row_w, group_e, n_used, dest


def kernel(x, meta_tokens, ln0_w, ln0_b, w_in, gk_fwd_w, gk_fwd_b, gk_bwd_w, gk_bwd_b, gla_norm_w, w_branch_a, rpb, meta_bias, w_branch_b, w_out, ln1_w, ln1_b, router_w, router_b, w1, b1, w2, b2, ln2_w, ln2_b):
    bsz, seq, d = x.shape
    depth = w_in.shape[0]
    alpha = (2 * depth) ** 0.25
    lp = FRONT + seq
    kdim = gk_fwd_w.shape[2]
    vdim = w_branch_a.shape[1]
    ndim = w_branch_b.shape[1]
    n_exp = router_w.shape[2]
    ff = w2.shape[2]
    assert seq % GRID_W == 0 and n_exp <= SMALL_N and 2 * GATE_RANK <= SMALL_N

    lr_col = 2 * kdim + 2 * vdim
    nq_col = lr_col
    ga_col = nq_col + 3 * ndim
    gb_col = ga_col + d
    w_main = jnp.concatenate([w_in[:, :, :lr_col], w_in[:, :, lr_col + 2 * GATE_RANK:]], axis=2).astype(bf16)
    w_lr = _split_rhs(jnp.pad(w_in[:, :, lr_col:lr_col + 2 * GATE_RANK], ((0, 0), (0, 0), (0, SMALL_N - 2 * GATE_RANK))))
    w_rt = _split_rhs(jnp.pad(router_w, ((0, 0), (0, 0), (0, SMALL_N - n_exp))))
    b_rt = jnp.pad(router_b, ((0, 0), (0, SMALL_N - n_exp)))
    zeros_small = jnp.zeros((SMALL_N,), f32)
    w_a = w_branch_a.astype(bf16)
    w_b = w_branch_b.astype(bf16)
    w_o = w_out.astype(bf16)
    w1p = _deinterleave_cast(w1.reshape(depth * n_exp, d, 2 * ff)).reshape(depth, n_exp, d, 2 * ff)
    b1p = jnp.concatenate([b1[..., 0::2], b1[..., 1::2]], axis=-1)
    w2b = w2.astype(bf16)
    bias_tbl = _na_bias_tables(rpb, seq // GRID_W)
    gw_f, gw_b = _split_rhs(gk_fwd_w), _split_rhs(gk_bwd_w)

    front = jnp.concatenate([jnp.zeros((PAD, d), x.dtype), meta_tokens.astype(x.dtype)], axis=0)
    xp = jnp.concatenate([jnp.broadcast_to(front[None], (bsz, FRONT, d)), x], axis=1).reshape(bsz * lp, d)
    h, hb, small = _embed_layer_norm(xp, ln0_w, ln0_b, w_lr[0], zeros_small)

    for l in range(depth):
        proj = _matmul(hb, w_main, l, bf16, "in_proj")
        lr_f = _split_lhs(small[:, :GATE_RANK])
        lr_b = _split_lhs(small[:, GATE_RANK:2 * GATE_RANK])
        o_bwd = _gla_direction(proj, lr_b, gw_b[l], gk_bwd_b[l], bsz, lp, reverse=True)
        o_a = _gla_direction(proj, lr_f, gw_f[l], gk_fwd_b[l], bsz, lp, reverse=False,
                             final_args=(o_bwd, gla_norm_w[l]))
        o_b = _neighborhood_attention(proj, bias_tbl[l], meta_bias[l], bsz, lp, nq_col)
        mixed = _branch_merge(o_a, o_b, w_a[l], w_b[l], proj, ga_col, gb_col)
        hs, top_e, gate = _mixer_layer_norm(mixed, w_o[l], h, alpha, ln1_w[l], ln1_b[l], w_rt[l], b_rt[l], n_exp)

        row_tok, row_w, group_e, n_used, dest = _route(top_e[:, :TOP_K], gate[:, :TOP_K], n_exp)
        y_rows = _expert_groups(hs, row_tok, row_w, group_e, n_used, w1p, b1p[l], w2b, b2[l], l)
        if l + 1 < depth:
            h, hb, small = _combine_layer_norm(y_rows, dest, hs, alpha, ln2_w[l], ln2_b[l], w_lr[l + 1], zeros_small)
        else:
            h, hb = _combine_layer_norm(y_rows, dest, hs, alpha, ln2_w[l], ln2_b[l])
    del ff, vdim
    return h.reshape(bsz, lp, d)[:, FRONT:]
```

```python
import functools

import numpy as np
import jax
import jax.numpy as jnp
from jax import lax
from jax.experimental import pallas as pl
from jax.experimental.pallas import tpu as pltpu

N_META = 16
GRID_W = 64
GLA_HEADS = 4
GATE_RANK = 16
GATE_NORMALIZER = 16.0
CHUNK = 64
NA_HEAD_DIM = 64
WIN_H = 8
WIN_W = 16
TOP_K = 4
SWIGLU_ALPHA = 1.702
SWIGLU_LIMIT = 7.0
GROUP_ROWS = 256
LN_EPS = 1e-5
RMS_EPS = 1e-6

FRONT = CHUNK
PAD = FRONT - N_META
SMALL_N = 128
NEG = -1e30
VMEM_LIMIT = 52 * 1024 * 1024
HIGHEST = lax.Precision.HIGHEST
NT = (((1,), (1,)), ((), ()))
TN = (((0,), (0,)), ((), ()))

f32 = jnp.float32
bf16 = jnp.bfloat16


def _params(*sem):
    return pltpu.CompilerParams(dimension_semantics=sem, vmem_limit_bytes=VMEM_LIMIT)


def _tile(n, prefs):
    for p in prefs:
        if n % p == 0:
            return p
    raise ValueError(f"no tile in {prefs} divides {n}")


LANES = 128


def _ln_math(z, w, b):
    mu = jnp.mean(z, axis=-1, keepdims=True)
    d = z - mu
    var = jnp.mean(d * d, axis=-1, keepdims=True)
    return d * lax.rsqrt(var + LN_EPS) * w + b


def _small_proj(h, pw_ref, pb_ref):
    h_hi = h.astype(bf16)
    h_lo = (h - h_hi.astype(f32)).astype(bf16)
    lhs = jnp.concatenate([h_hi, h_hi, h_lo], axis=1)
    return jnp.dot(lhs, pw_ref[...], preferred_element_type=f32) + pb_ref[...]


def _rows_to_slabs(ref, rows):
    n, d = rows.shape
    ns = d // LANES
    for c in range(ns):
        ref[pl.ds(c, n, stride=ns), :] = rows[:, c * LANES:(c + 1) * LANES]


def _slab_column(ref, lead, first_token, n, ns, c):
    return ref[lead + (pl.ds(first_token * ns + c, n, stride=ns), slice(None))]


def _embed_ln_kernel(x_ref, w_ref, b_ref, pw_ref, pb_ref, h_ref, hb_ref, p_ref):
    h = _ln_math(x_ref[...], w_ref[...], b_ref[...])
    h_ref[...] = h
    hb_ref[...] = h.astype(bf16)
    p_ref[...] = _small_proj(h, pw_ref, pb_ref)


def _embed_layer_norm(x, w, b, pw, pb):
    t, d = x.shape
    tm = _tile(t, (256, 128, 64))
    row = pl.BlockSpec((tm, d), lambda i: (i, 0))
    vec = pl.BlockSpec((1, d), lambda i: (0, 0))
    small = pl.BlockSpec((tm, SMALL_N), lambda i: (i, 0))
    return pl.pallas_call(
        _embed_ln_kernel, grid=(t // tm,),
        in_specs=[row, vec, vec, pl.BlockSpec((3 * d, SMALL_N), lambda i: (0, 0)),
                  pl.BlockSpec((1, SMALL_N), lambda i: (0, 0))],
        out_specs=[row, row, small],
        out_shape=[jax.ShapeDtypeStruct((t, d), f32), jax.ShapeDtypeStruct((t, d), bf16),
                   jax.ShapeDtypeStruct((t, SMALL_N), f32)],
        compiler_params=_params("parallel"), name="embed_layer_norm",
    )(x, w.reshape(1, d), b.reshape(1, d), pw, pb.reshape(1, SMALL_N))


def _mixer_ln_kernel(x_ref, mw_ref, res_ref, w_ref, b_ref, pw_ref, pb_ref, hs_ref, te_ref, tg_ref, *, alpha, n_exp):
    z = alpha * res_ref[...] + jnp.dot(x_ref[...], mw_ref[...], preferred_element_type=f32)
    h = _ln_math(z, w_ref[...], b_ref[...])
    _rows_to_slabs(hs_ref, h)
    lane = lax.broadcasted_iota(jnp.int32, (1, SMALL_N), 1)
    logits = jnp.where(lane < n_exp, _small_proj(h, pw_ref, pb_ref), NEG)
    top_e = jnp.zeros(logits.shape, jnp.int32)
    vals = []
    for kk in range(TOP_K):
        m = jnp.max(logits, axis=-1, keepdims=True)
        idx = jnp.min(jnp.where(logits == m, lane, SMALL_N), axis=-1, keepdims=True)
        top_e = jnp.where(lane == kk, idx, top_e)
        vals.append(m)
        logits = jnp.where(lane == idx, NEG, logits)
    ex = [jnp.exp(v - vals[0]) for v in vals]
    den = ex[0]
    for e_k in ex[1:]:
        den = den + e_k
    gate = jnp.zeros(logits.shape, f32)
    for kk in range(TOP_K):
        gate = jnp.where(lane == kk, ex[kk] / den, gate)
    te_ref[...] = top_e
    tg_ref[...] = gate


def _mixer_layer_norm(x, mm_w, res, alpha, w, b, pw, pb, n_exp):
    t, k = x.shape
    d = w.shape[0]
    ns = d // LANES
    tm = _tile(t, (256, 128, 64))
    row = pl.BlockSpec((tm, d), lambda i: (i, 0))
    vec = pl.BlockSpec((1, d), lambda i: (0, 0))
    small = pl.BlockSpec((tm, SMALL_N), lambda i: (i, 0))
    return pl.pallas_call(
        functools.partial(_mixer_ln_kernel, alpha=alpha, n_exp=n_exp), grid=(t // tm,),
        in_specs=[pl.BlockSpec((tm, k), lambda i: (i, 0)), pl.BlockSpec((k, d), lambda i: (0, 0)), row, vec, vec,
                  pl.BlockSpec((3 * d, SMALL_N), lambda i: (0, 0)), pl.BlockSpec((1, SMALL_N), lambda i: (0, 0))],
        out_specs=[pl.BlockSpec((tm * ns, LANES), lambda i: (i, 0)), small, small],
        out_shape=[jax.ShapeDtypeStruct((t * ns, LANES), f32), jax.ShapeDtypeStruct((t, SMALL_N), jnp.int32),
                   jax.ShapeDtypeStruct((t, SMALL_N), f32)],
        compiler_params=_params("parallel"), name="mixer_layer_norm",
    )(x, mm_w, res, w.reshape(1, d), b.reshape(1, d), pw, pb.reshape(1, SMALL_N))


def _mm_kernel(x_ref, w_ref, o_ref):
    o_ref[...] = jnp.dot(x_ref[...], w_ref[0], preferred_element_type=f32).astype(o_ref.dtype)


def _matmul(x, w, layer, out_dtype, name):
    m, k = x.shape
    n = w.shape[2]
    tm = _tile(m, (768, 512, 384, 256, 128))
    tn = _tile(n, (1024, 512, 256, 128))
    return pl.pallas_call(
        _mm_kernel,
        grid=(n // tn, m // tm),
        in_specs=[pl.BlockSpec((tm, k), lambda j, i: (i, 0)), pl.BlockSpec((1, k, tn), lambda j, i: (layer, 0, j))],
        out_specs=pl.BlockSpec((tm, tn), lambda j, i: (i, j)),
        out_shape=jax.ShapeDtypeStruct((m, n), out_dtype),
        compiler_params=_params("parallel", "parallel"), name=name,
    )(x, w)


LANE_PAIR = 256


def _deinterleave_kernel(x_ref, p_ref, o_ref):
    half = o_ref.shape[-1] // 2
    out_w = LANE_PAIR // 2
    for j in range(x_ref.shape[-1] // LANE_PAIR):
        xb = x_ref[0, :, j * LANE_PAIR:(j + 1) * LANE_PAIR].astype(bf16)
        y = jnp.dot(xb, p_ref[...], preferred_element_type=f32).astype(bf16)
        o_ref[0, :, j * out_w:(j + 1) * out_w] = y[:, :out_w]
        o_ref[0, :, half + j * out_w:half + (j + 1) * out_w] = y[:, out_w:]


def _deinterleave_cast(w):
    n, d, f2 = w.shape
    assert f2 % LANE_PAIR == 0
    tr = _tile(d, (1024, 512, 256, 128))
    perm = np.zeros((LANE_PAIR, LANE_PAIR), np.float32)
    idx = np.arange(LANE_PAIR // 2)
    perm[2 * idx, idx] = 1.0
    perm[2 * idx + 1, LANE_PAIR // 2 + idx] = 1.0
    return pl.pallas_call(
        _deinterleave_kernel,
        grid=(n, d // tr),
        in_specs=[pl.BlockSpec((1, tr, f2), lambda i, r: (i, r, 0)),
                  pl.BlockSpec((LANE_PAIR, LANE_PAIR), lambda i, r: (0, 0))],
        out_specs=pl.BlockSpec((1, tr, f2), lambda i, r: (i, r, 0)),
        out_shape=jax.ShapeDtypeStruct((n, d, f2), bf16),
        compiler_params=_params("parallel", "parallel"), name="deinterleave_cast",
    )(w, jnp.asarray(perm, bf16))


def _split_lhs(a):
    hi = a.astype(bf16)
    lo = (a - hi.astype(f32)).astype(bf16)
    return jnp.concatenate([hi, hi, lo], axis=1)


def _split_rhs(w):
    hi = w.astype(bf16)
    lo = (w - hi.astype(f32)).astype(bf16)
    return jnp.concatenate([hi, lo, hi], axis=-2)


def _branch_kernel(oa_ref, ob_ref, wa_ref, wb_ref, ga_ref, gb_ref, o_ref):
    ya = jnp.dot(oa_ref[...], wa_ref[...], preferred_element_type=f32)
    yb = jnp.dot(ob_ref[...], wb_ref[...], preferred_element_type=f32)
    mixed = jax.nn.sigmoid(ga_ref[...].astype(f32)) * ya + jax.nn.sigmoid(gb_ref[...].astype(f32)) * yb
    o_ref[...] = mixed.astype(o_ref.dtype)


def _branch_merge(o_a, o_b, w_a, w_b, proj, ga_col, gb_col):
    t, ka = o_a.shape
    kb = o_b.shape[1]
    d = w_a.shape[1]
    tm = _tile(t, (768, 512, 384, 256, 128))
    tn = 512
    assert ga_col % tn == 0 and gb_col % tn == 0 and d % tn == 0
    return pl.pallas_call(
        _branch_kernel,
        grid=(d // tn, t // tm),
        in_specs=[
            pl.BlockSpec((tm, ka), lambda j, i: (i, 0)),
            pl.BlockSpec((tm, kb), lambda j, i: (i, 0)),
            pl.BlockSpec((ka, tn), lambda j, i: (0, j)),
            pl.BlockSpec((kb, tn), lambda j, i: (0, j)),
            pl.BlockSpec((tm, tn), lambda j, i: (i, ga_col // tn + j)),
            pl.BlockSpec((tm, tn), lambda j, i: (i, gb_col // tn + j)),
        ],
        out_specs=pl.BlockSpec((tm, tn), lambda j, i: (i, j)),
        out_shape=jax.ShapeDtypeStruct((t, d), bf16),
        compiler_params=_params("parallel", "parallel"), name="branch_merge",
    )(o_a, o_b, w_a, w_b, proj, proj)


N_LEVELS = 6
DIAG_LEVEL = N_LEVELS + 1
KDEC_BLOCK = N_LEVELS + 1


def _gla_constants(reverse):
    c = CHUNK
    i = np.arange(c)[:, None]
    s = np.arange(c)[None, :]
    blocks = [(s >= i) if reverse else (s <= i)]
    level = np.zeros((c, c), np.int32)
    size = c
    for lv in range(1, N_LEVELS + 1):
        half = size // 2
        m = np.zeros((c, c), bool)
        for r in range(c):
            base = (r // size) * size
            first_half = (r % size) < half
            if not reverse:
                mid = base + half - 1
                if first_half:
                    m[r, r + 1:mid + 1] = True
                else:
                    m[r, mid + 1:r + 1] = True
            else:
                mid = base + half
                if first_half:
                    m[r, r:mid] = True
                else:
                    m[r, mid:r] = True
        blocks.append(m)
        same = (i // size) == (s // size)
        if not reverse:
            cond = same & ((i % size) >= half) & ((s % size) < half)
        else:
            cond = same & ((i % size) < half) & ((s % size) >= half)
        level[cond] = lv
        size = half
    level[np.eye(c, dtype=bool)] = DIAG_LEVEL
    blocks.append((s < i) if reverse else (s > i))
    return np.concatenate(blocks, axis=0).astype(np.float32), level


def _gla_kernel(*refs, reverse, cps, hps, final, dk):
    it = iter(refs)
    q_ref, k_ref, v_ref, lr_ref, gw_ref, gb_ref, ms_ref, lvl_ref = (next(it) for _ in range(8))
    if final:
        r_ref, ob_ref, nw_ref = next(it), next(it), next(it)
    o_ref = next(it)
    st_ref = next(it)
    dv = st_ref.shape[1]
    step = pl.program_id(2)
    nsteps = pl.num_programs(2)

    @pl.when(step == 0)
    def _():
        st_ref[...] = jnp.zeros_like(st_ref)

    blk = (nsteps - 1 - step) if reverse else step
    ms = ms_ref[...]
    lvl = lvl_ref[...]
    row = lax.broadcasted_iota(jnp.int32, (CHUNK, 1), 0)
    scale = dk ** -0.5
    order = list(range(cps))[::-1] if reverse else list(range(cps))
    rows = [slice(c * CHUNK, (c + 1) * CHUNK) for c in range(cps)]
    kcol = [slice(u * dk, (u + 1) * dk) for u in range(hps)]
    vcol = [slice(u * dv, (u + 1) * dv) for u in range(hps)]
    units = [(u, c) for c in order for u in range(hps)]

    q, k, v, e = {}, {}, {}, {}
    for c in order:
        valid = (row >= PAD) | (blk * cps + c > 0)
        x_all = jnp.dot(lr_ref[rows[c], :], gw_ref[...], preferred_element_type=f32) + gb_ref[...]
        for u in range(hps):
            q[u, c] = jnp.where(valid, q_ref[rows[c], kcol[u]].astype(f32) * scale, 0.0)
            k[u, c] = jnp.where(valid, k_ref[rows[c], kcol[u]].astype(f32), 0.0)
            v[u, c] = jnp.where(valid, v_ref[rows[c], vcol[u]], jnp.zeros((), bf16))
            x = x_all[:, kcol[u]]
            g = (jnp.minimum(x, 0.0) - jnp.log1p(jnp.exp(-jnp.abs(x)))) * (1.0 / GATE_NORMALIZER)
            g = jnp.where(valid, g, 0.0)
            g_hi = g.astype(bf16)
            g_lo = (g - g_hi.astype(f32)).astype(bf16)
            e[u, c] = jnp.dot(ms, jnp.concatenate([g_hi, g_lo], axis=0), preferred_element_type=f32)
    scores = {uc: jnp.zeros((CHUNK, CHUNK), f32) for uc in units}
    for lv in range(1, N_LEVELS + 1):
        for uc in units:
            w = jnp.exp(e[uc][lv * CHUNK:(lv + 1) * CHUNK])
            a = lax.dot_general((q[uc] * w).astype(bf16), (k[uc] * w).astype(bf16), NT, preferred_element_type=f32)
            scores[uc] = jnp.where(lvl == lv, a, scores[uc])
    qi, upd, total, o_intra = {}, {}, {}, {}
    for uc in units:
        a = lax.dot_general(q[uc].astype(bf16), k[uc].astype(bf16), NT, preferred_element_type=f32)
        scores[uc] = jnp.where(lvl == DIAG_LEVEL, a, scores[uc])
        cum = e[uc][0:CHUNK]
        qi[uc] = (q[uc] * jnp.exp(cum)).astype(bf16)
        kd = (k[uc] * jnp.exp(e[uc][KDEC_BLOCK * CHUNK:(KDEC_BLOCK + 1) * CHUNK])).astype(bf16)
        upd[uc] = lax.dot_general(v[uc], kd, TN, preferred_element_type=f32)
        total[uc] = jnp.exp(cum[0:1] if reverse else cum[CHUNK - 1:CHUNK])
    for uc in units:
        o_intra[uc] = jnp.dot(scores[uc].astype(bf16), v[uc], preferred_element_type=f32)

    st = [st_ref[u] for u in range(hps)]
    for u, c in units:
        o = lax.dot_general(qi[u, c], st[u].astype(bf16), NT, preferred_element_type=f32) + o_intra[u, c]
        st[u] = st[u] * total[u, c] + upd[u, c]
        if final:
            o = o + ob_ref[rows[c], vcol[u]]
            o = o * lax.rsqrt(jnp.mean(o * o, axis=-1, keepdims=True) + RMS_EPS) * nw_ref[...]
            o = o * jax.nn.silu(r_ref[rows[c], vcol[u]].astype(f32))
        o_ref[rows[c], vcol[u]] = o.astype(o_ref.dtype)
    for u in range(hps):
        st_ref[u] = st[u]


def _gla_direction(proj, lr, gw, gb, bsz, lp, reverse, final_args=None):
    t = proj.shape[0]
    heads = GLA_HEADS
    dk = gw.shape[1] // heads
    dv = 2 * dk
    n_chunks = lp // CHUNK
    cps = _tile(n_chunks, (3, 2, 1))
    rb = cps * CHUNK
    nsteps = n_chunks // cps
    final = final_args is not None
    ms, lvl = _gla_constants(reverse)
    ms = np.concatenate([ms, ms], axis=1)
    nblk = ms.shape[0]
    rank3 = lr.shape[1]

    def rblk(b, s):
        return b * nsteps + ((nsteps - 1 - s) if reverse else s)

    hps = 4 if heads % 4 == 0 else 1
    groups = heads // hps
    k_off = groups
    v_off = groups
    r_off = 2 * groups
    in_specs = [
        pl.BlockSpec((rb, hps * dk), lambda b, h, s: (rblk(b, s), h)),
        pl.BlockSpec((rb, hps * dk), lambda b, h, s: (rblk(b, s), k_off + h)),
        pl.BlockSpec((rb, hps * dv), lambda b, h, s: (rblk(b, s), v_off + h)),
        pl.BlockSpec((rb, rank3), lambda b, h, s: (rblk(b, s), 0)),
        pl.BlockSpec((rank3, hps * dk), lambda b, h, s: (0, h)),
        pl.BlockSpec((1, hps * dk), lambda b, h, s: (0, h)),
        pl.BlockSpec((nblk, 2 * CHUNK), lambda b, h, s: (0, 0)),
        pl.BlockSpec((CHUNK, CHUNK), lambda b, h, s: (0, 0)),
    ]
    ins = [proj, proj, proj, lr, gw, gb.reshape(1, -1), jnp.asarray(ms, bf16), jnp.asarray(lvl)]
    if final:
        o_other, nw = final_args
        in_specs += [
            pl.BlockSpec((rb, hps * dv), lambda b, h, s: (rblk(b, s), r_off + h)),
            pl.BlockSpec((rb, hps * dv), lambda b, h, s: (rblk(b, s), h)),
            pl.BlockSpec((1, dv), lambda b, h, s: (0, 0)),
        ]
        ins += [proj, o_other, nw.reshape(1, dv)]
    return pl.pallas_call(
        functools.partial(_gla_kernel, reverse=reverse, cps=cps, hps=hps, final=final, dk=dk),
        grid=(bsz, groups, nsteps),
        in_specs=in_specs,
        out_specs=pl.BlockSpec((rb, hps * dv), lambda b, h, s: (rblk(b, s), h)),
        out_shape=jax.ShapeDtypeStruct((t, heads * dv), bf16 if final else f32),
        scratch_shapes=[pltpu.VMEM((hps, dv, dk), f32)],
        compiler_params=_params("parallel", "parallel", "arbitrary"),
        name="gla_final" if final else "gla_partial",
    )(*ins)


N_VARIANTS = WIN_H


def _na_bias_tables(rpb, grid_rows):
    del grid_rows
    n_layers, heads = rpb.shape[0], rpb.shape[1]
    cols = np.arange(GRID_W)
    col_start = np.clip(cols - WIN_W // 2, 0, GRID_W - WIN_W)
    kc = np.arange(GRID_W)
    in_win = (kc[None, :] >= col_start[:, None]) & (kc[None, :] < col_start[:, None] + WIN_W)
    dc = np.clip(kc[None, :] - cols[:, None] + (WIN_W - 1), 0, 2 * WIN_W - 2)
    rpb_c = jnp.where(in_win[None, None, None], rpb[:, :, :, dc], NEG)
    tbl = jnp.stack([rpb_c[:, :, vi:vi + WIN_H] for vi in range(N_VARIANTS)], axis=2)
    tbl = tbl.reshape(n_layers, heads // 2, 2, N_VARIANTS, WIN_H, GRID_W, GRID_W)
    tbl = jnp.transpose(tbl, (0, 1, 3, 2, 5, 4, 6))
    return tbl.reshape(n_layers, heads // 2, N_VARIANTS, 2 * GRID_W, WIN_H * GRID_W)


def _na_kernel(q_ref, k_ref, v_ref, bias_ref, mb_ref, o_ref, *, bps, nsteps, grid_rows):
    lane = lax.broadcasted_iota(jnp.int32, (1, 2 * NA_HEAD_DIM), 1)
    head0 = lane < NA_HEAD_DIM
    km = k_ref[PAD:FRONT, :]
    vm = v_ref[PAD:FRONT, :]
    mb = mb_ref[0]
    mbias = jnp.concatenate([jnp.broadcast_to(mb[0:1], (GRID_W, N_META)),
                             jnp.broadcast_to(mb[1:2], (GRID_W, N_META))], axis=0)
    zero = jnp.zeros((), bf16)
    scale = NA_HEAD_DIM ** -0.5

    def stacked_queries(rows):
        q2 = q_ref[rows, :] * jnp.asarray(scale, bf16)
        return jnp.concatenate([jnp.where(head0, q2, zero), jnp.where(head0, zero, q2)], axis=0)

    def unstack(o):
        return jnp.where(head0, o[0:GRID_W], o[GRID_W:2 * GRID_W])

    row = lax.broadcasted_iota(jnp.int32, (GRID_W, 1), 0)
    blocks = range(bps)

    def step_body(step, carry):
        qq, vv, sl, sm, front, rows = {}, {}, {}, {}, {}, {}
        for j in blocks:
            g = step * bps + j
            r = g - 1
            r0 = jnp.clip(r - WIN_H // 2, 0, grid_rows - WIN_H)
            vi = jnp.clip(r0 - r + WIN_H // 2 + 3, 0, N_VARIANTS - 1)
            ks = pl.multiple_of((r0 + 1) * GRID_W, GRID_W)
            kk = k_ref[pl.ds(ks, WIN_H * GRID_W), :]
            vv[j] = v_ref[pl.ds(ks, WIN_H * GRID_W), :]
            rows[j] = pl.ds(pl.multiple_of(g * GRID_W, GRID_W), GRID_W)
            qq[j] = stacked_queries(rows[j])
            front[j] = g == 0
            sl[j] = lax.dot_general(qq[j], kk, NT, preferred_element_type=f32) + bias_ref[0, vi]
            sm[j] = lax.dot_general(qq[j], km, NT, preferred_element_type=f32) + mbias
        p_loc, p_meta, den = {}, {}, {}
        for j in blocks:
            s_loc = sl[j] + jnp.where(front[j], NEG, 0.0) if j == 0 else sl[j]
            mx = jnp.maximum(jnp.max(s_loc, axis=-1, keepdims=True), jnp.max(sm[j], axis=-1, keepdims=True))
            p_loc[j] = jnp.exp(s_loc - mx)
            p_meta[j] = jnp.exp(sm[j] - mx)
            den[j] = jnp.sum(p_loc[j], axis=-1, keepdims=True) + jnp.sum(p_meta[j], axis=-1, keepdims=True)
        for j in blocks:
            o = (jnp.dot(p_loc[j].astype(bf16), vv[j], preferred_element_type=f32)
                 + jnp.dot(p_meta[j].astype(bf16), vm, preferred_element_type=f32))
            o = unstack(o / den[j])
            if j == 0:
                o = jnp.where((row >= PAD) | jnp.logical_not(front[j]), o, 0.0)
            o_ref[rows[j], :] = o.astype(o_ref.dtype)
        return carry

    lax.fori_loop(0, nsteps, step_body, 0)


def _neighborhood_attention(proj, bias_tbl, meta_bias, bsz, lp, q_col):
    t = proj.shape[0]
    n_pairs = bias_tbl.shape[0]
    width = 2 * NA_HEAD_DIM
    grid_rows = (lp - FRONT) // GRID_W
    assert grid_rows >= WIN_H
    n_blocks = lp // GRID_W
    bps = _tile(n_blocks, (3, 2, 1))
    nsteps = n_blocks // bps
    qb = q_col // width
    return pl.pallas_call(
        functools.partial(_na_kernel, bps=bps, nsteps=nsteps, grid_rows=grid_rows),
        grid=(bsz, n_pairs),
        in_specs=[
            pl.BlockSpec((lp, width), lambda b, p: (b, qb + p)),
            pl.BlockSpec((lp, width), lambda b, p: (b, qb + n_pairs + p)),
            pl.BlockSpec((lp, width), lambda b, p: (b, qb + 2 * n_pairs + p)),
            pl.BlockSpec((1,) + bias_tbl.shape[1:], lambda b, p: (p, 0, 0, 0)),
            pl.BlockSpec((1, 2, N_META), lambda b, p: (p, 0, 0)),
        ],
        out_specs=pl.BlockSpec((lp, width), lambda b, p: (b, p)),
        out_shape=jax.ShapeDtypeStruct((t, n_pairs * width), bf16),
        compiler_params=_params("parallel", "parallel"),
        name="neighborhood_attention",
    )(proj, proj, proj, bias_tbl, meta_bias.reshape(n_pairs, 2, N_META))


ISSUE_UNROLL = 8


def _start_row_gather(src_hbm, idx_smem, slot, dst, sem, n, ns):
    base = slot * n

    def body(j, c):
        r0 = j * ISSUE_UNROLL
        for u in range(ISSUE_UNROLL):
            src_row = pl.multiple_of(idx_smem[base + r0 + u] * ns, ns)
            dst_row = pl.multiple_of((r0 + u) * ns, ns)
            pltpu.make_async_copy(src_hbm.at[pl.ds(src_row, ns), :], dst.at[slot, pl.ds(dst_row, ns), :],
                                  sem.at[slot]).start()
        return c

    lax.fori_loop(0, n // ISSUE_UNROLL, body, 0)


def _wait_row_gather(src_hbm, slot, dst, sem, n, ns):
    pltpu.make_async_copy(src_hbm.at[pl.ds(0, n * ns), :], dst.at[slot], sem.at[slot]).wait()


def _gather_pipeline(idx_hbm, src_hbm, idx_smem, buf, sem_idx, sem_rows, n, ns):
    g = pl.program_id(0)
    last = pl.num_programs(0) - 1
    slot = g % 2
    nxt = 1 - slot

    def idx_copy(step, s):
        return pltpu.make_async_copy(idx_hbm.at[step], idx_smem.at[pl.ds(s * n, n)], sem_idx.at[s])

    @pl.when(g == 0)
    def _():
        c = idx_copy(0, 0)
        c.start()
        c.wait()
        _start_row_gather(src_hbm, idx_smem, 0, buf, sem_rows, n, ns)

        @pl.when(last >= 1)
        def _():
            idx_copy(1, 1).start()

    @pl.when(g < last)
    def _():
        idx_copy(g + 1, nxt).wait()
        _start_row_gather(src_hbm, idx_smem, nxt, buf, sem_rows, n, ns)

    @pl.when(g + 1 < last)
    def _():
        idx_copy(g + 2, slot).start()

    _wait_row_gather(src_hbm, slot, buf, sem_rows, n, ns)
    return slot


def _expert_kernel(ge_ref, nu_ref, idx_hbm, x_hbm, w1_ref, b1_ref, w2_ref, b2_ref, rw_ref, y_ref,
                   idx_smem, xbuf, sem_idx, sem_rows):
    del ge_ref
    ns = xbuf.shape[1] // GROUP_ROWS
    slot = _gather_pipeline(idx_hbm, x_hbm, idx_smem, xbuf, sem_idx, sem_rows, GROUP_ROWS, ns)
    g = pl.program_id(0)
    ff = w2_ref.shape[2]

    @pl.when(g < nu_ref[0])
    def _():
        x = jnp.concatenate([_slab_column(xbuf, (slot,), 0, GROUP_ROWS, ns, c) for c in range(ns)], axis=1).astype(bf16)
        h = jnp.dot(x, w1_ref[0, 0], preferred_element_type=f32) + b1_ref[0]
        x_glu = jnp.minimum(h[:, :ff], SWIGLU_LIMIT)
        x_lin = jnp.clip(h[:, ff:], -SWIGLU_LIMIT, SWIGLU_LIMIT)
        act = x_glu * jax.nn.sigmoid(SWIGLU_ALPHA * x_glu) * (x_lin + 1.0)
        y = jnp.dot(act.astype(bf16), w2_ref[0, 0], preferred_element_type=f32) + b2_ref[0]
        _rows_to_slabs(y_ref, y * rw_ref[...])

    @pl.when(g >= nu_ref[0])
    def _():
        y_ref[...] = jnp.zeros_like(y_ref)


def _expert_groups(hs, row_tok, row_w, group_e, n_used, w1, b1, w2, b2, layer):
    _, n_exp, d, f2 = w1.shape
    nslab = d // LANES
    n_groups = row_tok.shape[0]
    ff = f2 // 2
    grid_spec = pltpu.PrefetchScalarGridSpec(
        num_scalar_prefetch=2,
        grid=(n_groups,),
        in_specs=[
            pl.BlockSpec(memory_space=pl.ANY),
            pl.BlockSpec(memory_space=pl.ANY),
            pl.BlockSpec((1, 1, d, f2), lambda g, ge, nu: (layer, ge[g], 0, 0)),
            pl.BlockSpec((1, 1, f2), lambda g, ge, nu: (ge[g], 0, 0)),
            pl.BlockSpec((1, 1, ff, d), lambda g, ge, nu: (layer, ge[g], 0, 0)),
            pl.BlockSpec((1, 1, d), lambda g, ge, nu: (ge[g], 0, 0)),
            pl.BlockSpec((GROUP_ROWS, 1), lambda g, ge, nu: (g, 0)),
        ],
        out_specs=pl.BlockSpec((GROUP_ROWS * nslab, LANES), lambda g, ge, nu: (g, 0)),
        scratch_shapes=[
            pltpu.SMEM((2 * GROUP_ROWS,), jnp.int32),
            pltpu.VMEM((2, GROUP_ROWS * nslab, LANES), f32),
            pltpu.SemaphoreType.DMA((2,)),
            pltpu.SemaphoreType.DMA((2,)),
        ],
    )
    return pl.pallas_call(
        _expert_kernel, grid_spec=grid_spec,
        out_shape=jax.ShapeDtypeStruct((n_groups * GROUP_ROWS * nslab, LANES), f32),
        compiler_params=_params("arbitrary"), name="expert_groups",
    )(group_e, n_used, row_tok, hs, w1, b1.reshape(n_exp, 1, f2), w2, b2.reshape(n_exp, 1, d),
      row_w.reshape(n_groups * GROUP_ROWS, 1))


COMBINE_TOKENS = 128


def _combine_ln_kernel(*refs, alpha, has_proj):
    it = iter(refs)
    idx_hbm, y_hbm, res_ref, w_ref, b_ref = (next(it) for _ in range(5))
    pw_ref, pb_ref = (next(it), next(it)) if has_proj else (None, None)
    h_ref, hb_ref = next(it), next(it)
    p_ref = next(it) if has_proj else None
    idx_smem, buf, sem_idx, sem_rows = (next(it) for _ in range(4))
    tc = COMBINE_TOKENS
    ns = res_ref.shape[0] // tc
    slot = _gather_pipeline(idx_hbm, y_hbm, idx_smem, buf, sem_idx, sem_rows, TOP_K * tc, ns)
    cols = []
    for c in range(ns):
        acc = _slab_column(buf, (slot,), 0, tc, ns, c)
        for kk in range(1, TOP_K):
            acc = acc + _slab_column(buf, (slot,), kk * tc, tc, ns, c)
        cols.append(alpha * _slab_column(res_ref, (), 0, tc, ns, c) + acc)
    h = _ln_math(jnp.concatenate(cols, axis=1), w_ref[...], b_ref[...])
    h_ref[...] = h
    hb_ref[...] = h.astype(bf16)
    if has_proj:
        p_ref[...] = _small_proj(h, pw_ref, pb_ref)


def _combine_layer_norm(y_rows, dest, hs, alpha, w, b, pw=None, pb=None):
    d = w.shape[0]
    nslab = d // LANES
    t = hs.shape[0] // nslab
    tc = COMBINE_TOKENS
    n_tiles = t // tc
    has_proj = pw is not None
    idx = dest.reshape(n_tiles, tc, TOP_K).transpose(0, 2, 1).reshape(n_tiles, TOP_K * tc)
    row = pl.BlockSpec((tc, d), lambda i: (i, 0))
    vec = pl.BlockSpec((1, d), lambda i: (0, 0))
    ins = [idx, y_rows, hs, w.reshape(1, d), b.reshape(1, d)]
    in_specs = [pl.BlockSpec(memory_space=pl.ANY), pl.BlockSpec(memory_space=pl.ANY),
                pl.BlockSpec((tc * nslab, LANES), lambda i: (i, 0)), vec, vec]
    outs = [jax.ShapeDtypeStruct((t, d), f32), jax.ShapeDtypeStruct((t, d), bf16)]
    out_specs = [row, row]
    if has_proj:
        ins += [pw, pb.reshape(1, SMALL_N)]
        in_specs += [pl.BlockSpec((3 * d, SMALL_N), lambda i: (0, 0)), pl.BlockSpec((1, SMALL_N), lambda i: (0, 0))]
        outs.append(jax.ShapeDtypeStruct((t, SMALL_N), f32))
        out_specs.append(pl.BlockSpec((tc, SMALL_N), lambda i: (i, 0)))
    return pl.pallas_call(
        functools.partial(_combine_ln_kernel, alpha=alpha, has_proj=has_proj),
        grid=(n_tiles,), in_specs=in_specs, out_specs=out_specs, out_shape=outs,
        scratch_shapes=[
            pltpu.SMEM((2 * TOP_K * tc,), jnp.int32),
            pltpu.VMEM((2, TOP_K * tc * nslab, LANES), f32),
            pltpu.SemaphoreType.DMA((2,)),
            pltpu.SemaphoreType.DMA((2,)),
        ],
        compiler_params=_params("arbitrary"), name="combine_layer_norm",
    )(*ins)


def _route(top_e, gate, n_exp):
    t = top_e.shape[0]
    n = t * TOP_K
    flat_e = top_e.reshape(-1)
    pos = jnp.arange(n, dtype=jnp.int32)
    e_sorted, order = lax.sort((flat_e, pos), num_keys=1, is_stable=True)
    _, inv = lax.sort((order, pos), num_keys=1)
    bounds = jnp.searchsorted(e_sorted, jnp.arange(n_exp + 1, dtype=jnp.int32), side="left",
                              method="scan_unrolled").astype(jnp.int32)
    start = bounds[:n_exp]
    counts = bounds[1:] - start
    padded = (counts + GROUP_ROWS - 1) // GROUP_ROWS * GROUP_ROWS
    pend = jnp.cumsum(padded)
    pstart = pend - padded
    dest = (inv + (pstart - start)[flat_e]).reshape(t, TOP_K)
    n_groups = -(-n // GROUP_ROWS) + n_exp
    group_row0 = jnp.arange(n_groups, dtype=jnp.int32) * GROUP_ROWS
    group_e = jnp.minimum(jnp.sum(pend[None, :] <= group_row0[:, None], axis=1, dtype=jnp.int32), n_exp - 1)
    within = (group_row0 - pstart[group_e])[:, None] + jnp.arange(GROUP_ROWS, dtype=jnp.int32)[None, :]
    live = (within < counts[group_e][:, None]) & (group_row0 < pend[n_exp - 1])[:, None]
    src = jnp.clip(start[group_e][:, None] + within, 0, n - 1)
    pair = order[src]
    row_tok = jnp.where(live, pair // TOP_K, 0).astype(jnp.int32)
    row_w = jnp.where(live, gate.reshape(-1)[pair], 0.0)
    n_used = (pend[n_exp - 1] // GROUP_ROWS).astype(jnp.int32).reshape(1)
    return row_tok, row_w, group_e, n_used, dest


def kernel(x, meta_tokens, ln0_w, ln0_b, w_in, gk_fwd_w, gk_fwd_b, gk_bwd_w, gk_bwd_b, gla_norm_w, w_branch_a, rpb, meta_bias, w_branch_b, w_out, ln1_w, ln1_b, router_w, router_b, w1, b1, w2, b2, ln2_w, ln2_b):
    bsz, seq, d = x.shape
    depth = w_in.shape[0]
    alpha = (2 * depth) ** 0.25
    lp = FRONT + seq
    kdim = gk_fwd_w.shape[2]
    vdim = w_branch_a.shape[1]
    ndim = w_branch_b.shape[1]
    n_exp = router_w.shape[2]
    ff = w2.shape[2]
    assert seq % GRID_W == 0 and n_exp <= SMALL_N and 2 * GATE_RANK <= SMALL_N

    lr_col = 2 * kdim + 2 * vdim
    nq_col = lr_col
    ga_col = nq_col + 3 * ndim
    gb_col = ga_col + d
    w_main = jnp.concatenate([w_in[:, :, :lr_col], w_in[:, :, lr_col + 2 * GATE_RANK:]], axis=2).astype(bf16)
    w_lr = _split_rhs(jnp.pad(w_in[:, :, lr_col:lr_col + 2 * GATE_RANK], ((0, 0), (0, 0), (0, SMALL_N - 2 * GATE_RANK))))
    w_rt = _split_rhs(jnp.pad(router_w, ((0, 0), (0, 0), (0, SMALL_N - n_exp))))
    b_rt = jnp.pad(router_b, ((0, 0), (0, SMALL_N - n_exp)))
    zeros_small = jnp.zeros((SMALL_N,), f32)
    w_a = w_branch_a.astype(bf16)
    w_b = w_branch_b.astype(bf16)
    w_o = w_out.astype(bf16)
    w1p = _deinterleave_cast(w1.reshape(depth * n_exp, d, 2 * ff)).reshape(depth, n_exp, d, 2 * ff)
    b1p = jnp.concatenate([b1[..., 0::2], b1[..., 1::2]], axis=-1)
    w2b = w2.astype(bf16)
    bias_tbl = _na_bias_tables(rpb, seq // GRID_W)
    gw_f, gw_b = _split_rhs(gk_fwd_w), _split_rhs(gk_bwd_w)

    front = jnp.concatenate([jnp.zeros((PAD, d), x.dtype), meta_tokens.astype(x.dtype)], axis=0)
    xp = jnp.concatenate([jnp.broadcast_to(front[None], (bsz, FRONT, d)), x], axis=1).reshape(bsz * lp, d)
    h, hb, small = _embed_layer_norm(xp, ln0_w, ln0_b, w_lr[0], zeros_small)

    for l in range(depth):
        proj = _matmul(hb, w_main, l, bf16, "in_proj")
        lr_f = _split_lhs(small[:, :GATE_RANK])
        lr_b = _split_lhs(small[:, GATE_RANK:2 * GATE_RANK])
        o_bwd = _gla_direction(proj, lr_b, gw_b[l], gk_bwd_b[l], bsz, lp, reverse=True)
        o_a = _gla_direction(proj, lr_f, gw_f[l], gk_fwd_b[l], bsz, lp, reverse=False,
                             final_args=(o_bwd, gla_norm_w[l]))
        o_b = _neighborhood_attention(proj, bias_tbl[l], meta_bias[l], bsz, lp, nq_col)
        mixed = _branch_merge(o_a, o_b, w_a[l], w_b[l], proj, ga_col, gb_col)
        hs, top_e, gate = _mixer_layer_norm(mixed, w_o[l], h, alpha, ln1_w[l], ln1_b[l], w_rt[l], b_rt[l], n_exp)

        row_tok, row_w, group_e, n_used, dest = _route(top_e[:, :TOP_K], gate[:, :TOP_K], n_exp)
        y_rows = _expert_groups(hs, row_tok, row_w, group_e, n_used, w1p, b1p[l], w2b, b2[l], l)
        if l + 1 < depth:
            h, hb, small = _combine_layer_norm(y_rows, dest, hs, alpha, ln2_w[l], ln2_b[l], w_lr[l + 1], zeros_small)
        else:
            h, hb = _combine_layer_norm(y_rows, dest, hs, alpha, ln2_w[l], ln2_b[l])
    del ff, vdim
    return h.reshape(bsz, lp, d)[:, FRONT:]
```

```python
import functools

import numpy as np
import jax
import jax.numpy as jnp
from jax import lax
from jax.experimental import pallas as pl
from jax.experimental.pallas import tpu as pltpu

N_META = 16
GRID_W = 64
GLA_HEADS = 4
GATE_RANK = 16
GATE_NORMALIZER = 16.0
CHUNK = 64
NA_HEAD_DIM = 64
WIN_H = 8
WIN_W = 16
TOP_K = 4
SWIGLU_ALPHA = 1.702
SWIGLU_LIMIT = 7.0
GROUP_ROWS = 256
LN_EPS = 1e-5
RMS_EPS = 1e-6

FRONT = CHUNK
PAD = FRONT - N_META
SMALL_N = 128
NEG = -1e30
VMEM_LIMIT = 52 * 1024 * 1024
HIGHEST = lax.Precision.HIGHEST
NT = (((1,), (1,)), ((), ()))
TN = (((0,), (0,)), ((), ()))

f32 = jnp.float32
bf16 = jnp.bfloat16


def _params(*sem):
    return pltpu.CompilerParams(dimension_semantics=sem, vmem_limit_bytes=VMEM_LIMIT)


def _tile(n, prefs):
    for p in prefs:
        if n % p == 0:
            return p
    raise ValueError(f"no tile in {prefs} divides {n}")


LANES = 128


def _ln_math(z, w, b):
    mu = jnp.mean(z, axis=-1, keepdims=True)
    d = z - mu
    var = jnp.mean(d * d, axis=-1, keepdims=True)
    return d * lax.rsqrt(var + LN_EPS) * w + b


def _small_proj(h, pw_ref, pb_ref):
    h_hi = h.astype(bf16)
    h_lo = (h - h_hi.astype(f32)).astype(bf16)
    lhs = jnp.concatenate([h_hi, h_hi, h_lo], axis=1)
    return jnp.dot(lhs, pw_ref[...], preferred_element_type=f32) + pb_ref[...]


def _rows_to_slabs(ref, rows):
    n, d = rows.shape
    ns = d // LANES
    for c in range(ns):
        ref[pl.ds(c, n, stride=ns), :] = rows[:, c * LANES:(c + 1) * LANES]


def _slab_column(ref, lead, first_token, n, ns, c):
    return ref[lead + (pl.ds(first_token * ns + c, n, stride=ns), slice(None))]


def _embed_ln_kernel(x_ref, w_ref, b_ref, pw_ref, pb_ref, h_ref, hb_ref, p_ref):
    h = _ln_math(x_ref[...], w_ref[...], b_ref[...])
    h_ref[...] = h
    hb_ref[...] = h.astype(bf16)
    p_ref[...] = _small_proj(h, pw_ref, pb_ref)


def _embed_layer_norm(x, w, b, pw, pb):
    t, d = x.shape
    tm = _tile(t, (256, 128, 64))
    row = pl.BlockSpec((tm, d), lambda i: (i, 0))
    vec = pl.BlockSpec((1, d), lambda i: (0, 0))
    small = pl.BlockSpec((tm, SMALL_N), lambda i: (i, 0))
    return pl.pallas_call(
        _embed_ln_kernel, grid=(t // tm,),
        in_specs=[row, vec, vec, pl.BlockSpec((3 * d, SMALL_N), lambda i: (0, 0)),
                  pl.BlockSpec((1, SMALL_N), lambda i: (0, 0))],
        out_specs=[row, row, small],
        out_shape=[jax.ShapeDtypeStruct((t, d), f32), jax.ShapeDtypeStruct((t, d), bf16),
                   jax.ShapeDtypeStruct((t, SMALL_N), f32)],
        compiler_params=_params("parallel"), name="embed_layer_norm",
    )(x, w.reshape(1, d), b.reshape(1, d), pw, pb.reshape(1, SMALL_N))


def _mixer_ln_kernel(x_ref, mw_ref, res_ref, w_ref, b_ref, pw_ref, pb_ref, hs_ref, te_ref, tg_ref, *, alpha, n_exp):
    z = alpha * res_ref[...] + jnp.dot(x_ref[...], mw_ref[...], preferred_element_type=f32)
    h = _ln_math(z, w_ref[...], b_ref[...])
    _rows_to_slabs(hs_ref, h)
    lane = lax.broadcasted_iota(jnp.int32, (1, SMALL_N), 1)
    logits = jnp.where(lane < n_exp, _small_proj(h, pw_ref, pb_ref), NEG)
    top_e = jnp.zeros(logits.shape, jnp.int32)
    vals = []
    for kk in range(TOP_K):
        m = jnp.max(logits, axis=-1, keepdims=True)
        idx = jnp.min(jnp.where(logits == m, lane, SMALL_N), axis=-1, keepdims=True)
        top_e = jnp.where(lane == kk, idx, top_e)
        vals.append(m)
        logits = jnp.where(lane == idx, NEG, logits)
    ex = [jnp.exp(v - vals[0]) for v in vals]
    den = ex[0]
    for e_k in ex[1:]:
        den = den + e_k
    gate = jnp.zeros(logits.shape, f32)
    for kk in range(TOP_K):
        gate = jnp.where(lane == kk, ex[kk] / den, gate)
    te_ref[...] = top_e
    tg_ref[...] = gate


def _mixer_layer_norm(x, mm_w, res, alpha, w, b, pw, pb, n_exp):
    t, k = x.shape
    d = w.shape[0]
    ns = d // LANES
    tm = _tile(t, (256, 128, 64))
    row = pl.BlockSpec((tm, d), lambda i: (i, 0))
    vec = pl.BlockSpec((1, d), lambda i: (0, 0))
    small = pl.BlockSpec((tm, SMALL_N), lambda i: (i, 0))
    return pl.pallas_call(
        functools.partial(_mixer_ln_kernel, alpha=alpha, n_exp=n_exp), grid=(t // tm,),
        in_specs=[pl.BlockSpec((tm, k), lambda i: (i, 0)), pl.BlockSpec((k, d), lambda i: (0, 0)), row, vec, vec,
                  pl.BlockSpec((3 * d, SMALL_N), lambda i: (0, 0)), pl.BlockSpec((1, SMALL_N), lambda i: (0, 0))],
        out_specs=[pl.BlockSpec((tm * ns, LANES), lambda i: (i, 0)), small, small],
        out_shape=[jax.ShapeDtypeStruct((t * ns, LANES), f32), jax.ShapeDtypeStruct((t, SMALL_N), jnp.int32),
                   jax.ShapeDtypeStruct((t, SMALL_N), f32)],
        compiler_params=_params("parallel"), name="mixer_layer_norm",
    )(x, mm_w, res, w.reshape(1, d), b.reshape(1, d), pw, pb.reshape(1, SMALL_N))


def _mm_kernel(x_ref, w_ref, o_ref):
    o_ref[...] = jnp.dot(x_ref[...], w_ref[0], preferred_element_type=f32).astype(o_ref.dtype)


def _matmul(x, w, layer, out_dtype, name):
    m, k = x.shape
    n = w.shape[2]
    tm = _tile(m, (768, 512, 384, 256, 128))
    tn = _tile(n, (1024, 512, 256, 128))
    return pl.pallas_call(
        _mm_kernel,
        grid=(n // tn, m // tm),
        in_specs=[pl.BlockSpec((tm, k), lambda j, i: (i, 0)), pl.BlockSpec((1, k, tn), lambda j, i: (layer, 0, j))],
        out_specs=pl.BlockSpec((tm, tn), lambda j, i: (i, j)),
        out_shape=jax.ShapeDtypeStruct((m, n), out_dtype),
        compiler_params=_params("parallel", "parallel"), name=name,
    )(x, w)


LANE_PAIR = 256


def _deinterleave_kernel(x_ref, p_ref, o_ref):
    half = o_ref.shape[-1] // 2
    out_w = LANE_PAIR // 2
    for j in range(x_ref.shape[-1] // LANE_PAIR):
        xb = x_ref[0, :, j * LANE_PAIR:(j + 1) * LANE_PAIR].astype(bf16)
        y = jnp.dot(xb, p_ref[...], preferred_element_type=f32).astype(bf16)
        o_ref[0, :, j * out_w:(j + 1) * out_w] = y[:, :out_w]
        o_ref[0, :, half + j * out_w:half + (j + 1) * out_w] = y[:, out_w:]


def _deinterleave_cast(w):
    n, d, f2 = w.shape
    assert f2 % LANE_PAIR == 0
    tr = _tile(d, (1024, 512, 256, 128))
    perm = np.zeros((LANE_PAIR, LANE_PAIR), np.float32)
    idx = np.arange(LANE_PAIR // 2)
    perm[2 * idx, idx] = 1.0
    perm[2 * idx + 1, LANE_PAIR // 2 + idx] = 1.0
    return pl.pallas_call(
        _deinterleave_kernel,
        grid=(n, d // tr),
        in_specs=[pl.BlockSpec((1, tr, f2), lambda i, r: (i, r, 0)),
                  pl.BlockSpec((LANE_PAIR, LANE_PAIR), lambda i, r: (0, 0))],
        out_specs=pl.BlockSpec((1, tr, f2), lambda i, r: (i, r, 0)),
        out_shape=jax.ShapeDtypeStruct((n, d, f2), bf16),
        compiler_params=_params("parallel", "parallel"), name="deinterleave_cast",
    )(w, jnp.asarray(perm, bf16))


def _split_lhs(a):
    hi = a.astype(bf16)
    lo = (a - hi.astype(f32)).astype(bf16)
    return jnp.concatenate([hi, hi, lo], axis=1)


def _split_rhs(w):
    hi = w.astype(bf16)
    lo = (w - hi.astype(f32)).astype(bf16)
    return jnp.concatenate([hi, lo, hi], axis=-2)


def _branch_kernel(oa_ref, ob_ref, wa_ref, wb_ref, ga_ref, gb_ref, o_ref):
    ya = jnp.dot(oa_ref[...], wa_ref[...], preferred_element_type=f32)
    yb = jnp.dot(ob_ref[...], wb_ref[...], preferred_element_type=f32)
    mixed = jax.nn.sigmoid(ga_ref[...].astype(f32)) * ya + jax.nn.sigmoid(gb_ref[...].astype(f32)) * yb
    o_ref[...] = mixed.astype(o_ref.dtype)


def _branch_merge(o_a, o_b, w_a, w_b, proj, ga_col, gb_col):
    t, ka = o_a.shape
    kb = o_b.shape[1]
    d = w_a.shape[1]
    tm = _tile(t, (768, 512, 384, 256, 128))
    tn = 1024
    assert ga_col % tn == 0 and gb_col % tn == 0 and d % tn == 0
    return pl.pallas_call(
        _branch_kernel,
        grid=(d // tn, t // tm),
        in_specs=[
            pl.BlockSpec((tm, ka), lambda j, i: (i, 0)),
            pl.BlockSpec((tm, kb), lambda j, i: (i, 0)),
            pl.BlockSpec((ka, tn), lambda j, i: (0, j)),
            pl.BlockSpec((kb, tn), lambda j, i: (0, j)),
            pl.BlockSpec((tm, tn), lambda j, i: (i, ga_col // tn + j)),
            pl.BlockSpec((tm, tn), lambda j, i: (i, gb_col // tn + j)),
        ],
        out_specs=pl.BlockSpec((tm, tn), lambda j, i: (i, j)),
        out_shape=jax.ShapeDtypeStruct((t, d), bf16),
        compiler_params=_params("parallel", "parallel"), name="branch_merge",
    )(o_a, o_b, w_a, w_b, proj, proj)


N_LEVELS = 6
DIAG_LEVEL = N_LEVELS + 1
KDEC_BLOCK = N_LEVELS + 1


def _gla_constants(reverse):
    c = CHUNK
    i = np.arange(c)[:, None]
    s = np.arange(c)[None, :]
    blocks = [(s >= i) if reverse else (s <= i)]
    level = np.zeros((c, c), np.int32)
    size = c
    for lv in range(1, N_LEVELS + 1):
        half = size // 2
        m = np.zeros((c, c), bool)
        for r in range(c):
            base = (r // size) * size
            first_half = (r % size) < half
            if not reverse:
                mid = base + half - 1
                if first_half:
                    m[r, r + 1:mid + 1] = True
                else:
                    m[r, mid + 1:r + 1] = True
            else:
                mid = base + half
                if first_half:
                    m[r, r:mid] = True
                else:
                    m[r, mid:r] = True
        blocks.append(m)
        same = (i // size) == (s // size)
        if not reverse:
            cond = same & ((i % size) >= half) & ((s % size) < half)
        else:
            cond = same & ((i % size) < half) & ((s % size) >= half)
        level[cond] = lv
        size = half
    level[np.eye(c, dtype=bool)] = DIAG_LEVEL
    blocks.append((s < i) if reverse else (s > i))
    return np.concatenate(blocks, axis=0).astype(np.float32), level


def _gla_kernel(*refs, reverse, cps, hps, final, dk):
    it = iter(refs)
    q_ref, k_ref, v_ref, lr_ref, gw_ref, gb_ref, ms_ref, lvl_ref = (next(it) for _ in range(8))
    if final:
        r_ref, ob_ref, nw_ref = next(it), next(it), next(it)
    o_ref = next(it)
    st_ref = next(it)
    dv = st_ref.shape[1]
    step = pl.program_id(2)
    nsteps = pl.num_programs(2)

    @pl.when(step == 0)
    def _():
        st_ref[...] = jnp.zeros_like(st_ref)

    blk = (nsteps - 1 - step) if reverse else step
    ms = ms_ref[...]
    lvl = lvl_ref[...]
    row = lax.broadcasted_iota(jnp.int32, (CHUNK, 1), 0)
    scale = dk ** -0.5
    order = list(range(cps))[::-1] if reverse else list(range(cps))
    rows = [slice(c * CHUNK, (c + 1) * CHUNK) for c in range(cps)]
    kcol = [slice(u * dk, (u + 1) * dk) for u in range(hps)]
    vcol = [slice(u * dv, (u + 1) * dv) for u in range(hps)]
    units = [(u, c) for c in order for u in range(hps)]

    q, k, v, e = {}, {}, {}, {}
    for c in order:
        valid = (row >= PAD) | (blk * cps + c > 0)
        x_all = jnp.dot(lr_ref[rows[c], :], gw_ref[...], preferred_element_type=f32) + gb_ref[...]
        for u in range(hps):
            q[u, c] = jnp.where(valid, q_ref[rows[c], kcol[u]].astype(f32) * scale, 0.0)
            k[u, c] = jnp.where(valid, k_ref[rows[c], kcol[u]].astype(f32), 0.0)
            v[u, c] = jnp.where(valid, v_ref[rows[c], vcol[u]], jnp.zeros((), bf16))
            x = x_all[:, kcol[u]]
            g = (jnp.minimum(x, 0.0) - jnp.log1p(jnp.exp(-jnp.abs(x)))) * (1.0 / GATE_NORMALIZER)
            g = jnp.where(valid, g, 0.0)
            g_hi = g.astype(bf16)
            g_lo = (g - g_hi.astype(f32)).astype(bf16)
            e[u, c] = jnp.dot(ms, jnp.concatenate([g_hi, g_lo], axis=0), preferred_element_type=f32)
    scores = {uc: jnp.zeros((CHUNK, CHUNK), f32) for uc in units}
    for lv in range(1, N_LEVELS + 1):
        for uc in units:
            w = jnp.exp(e[uc][lv * CHUNK:(lv + 1) * CHUNK])
            a = lax.dot_general((q[uc] * w).astype(bf16), (k[uc] * w).astype(bf16), NT, preferred_element_type=f32)
            scores[uc] = jnp.where(lvl == lv, a, scores[uc])
    qi, upd, total, o_intra = {}, {}, {}, {}
    for uc in units:
        a = lax.dot_general(q[uc].astype(bf16), k[uc].astype(bf16), NT, preferred_element_type=f32)
        scores[uc] = jnp.where(lvl == DIAG_LEVEL, a, scores[uc])
        cum = e[uc][0:CHUNK]
        qi[uc] = (q[uc] * jnp.exp(cum)).astype(bf16)
        kd = (k[uc] * jnp.exp(e[uc][KDEC_BLOCK * CHUNK:(KDEC_BLOCK + 1) * CHUNK])).astype(bf16)
        upd[uc] = lax.dot_general(v[uc], kd, TN, preferred_element_type=f32)
        total[uc] = jnp.exp(cum[0:1] if reverse else cum[CHUNK - 1:CHUNK])
    for uc in units:
        o_intra[uc] = jnp.dot(scores[uc].astype(bf16), v[uc], preferred_element_type=f32)

    st = [st_ref[u] for u in range(hps)]
    for u, c in units:
        o = lax.dot_general(qi[u, c], st[u].astype(bf16), NT, preferred_element_type=f32) + o_intra[u, c]
        st[u] = st[u] * total[u, c] + upd[u, c]
        if final:
            o = o + ob_ref[rows[c], vcol[u]]
            o = o * lax.rsqrt(jnp.mean(o * o, axis=-1, keepdims=True) + RMS_EPS) * nw_ref[...]
            o = o * jax.nn.silu(r_ref[rows[c], vcol[u]].astype(f32))
        o_ref[rows[c], vcol[u]] = o.astype(o_ref.dtype)
    for u in range(hps):
        st_ref[u] = st[u]


def _gla_direction(proj, lr, gw, gb, bsz, lp, reverse, final_args=None):
    t = proj.shape[0]
    heads = GLA_HEADS
    dk = gw.shape[1] // heads
    dv = 2 * dk
    n_chunks = lp // CHUNK
    cps = _tile(n_chunks, (3, 2, 1))
    rb = cps * CHUNK
    nsteps = n_chunks // cps
    final = final_args is not None
    ms, lvl = _gla_constants(reverse)
    ms = np.concatenate([ms, ms], axis=1)
    nblk = ms.shape[0]
    rank3 = lr.shape[1]

    def rblk(b, s):
        return b * nsteps + ((nsteps - 1 - s) if reverse else s)

    hps = 4 if heads % 4 == 0 else 1
    groups = heads // hps
    k_off = groups
    v_off = groups
    r_off = 2 * groups
    in_specs = [
        pl.BlockSpec((rb, hps * dk), lambda b, h, s: (rblk(b, s), h)),
        pl.BlockSpec((rb, hps * dk), lambda b, h, s: (rblk(b, s), k_off + h)),
        pl.BlockSpec((rb, hps * dv), lambda b, h, s: (rblk(b, s), v_off + h)),
        pl.BlockSpec((rb, rank3), lambda b, h, s: (rblk(b, s), 0)),
        pl.BlockSpec((rank3, hps * dk), lambda b, h, s: (0, h)),
        pl.BlockSpec((1, hps * dk), lambda b, h, s: (0, h)),
        pl.BlockSpec((nblk, 2 * CHUNK), lambda b, h, s: (0, 0)),
        pl.BlockSpec((CHUNK, CHUNK), lambda b, h, s: (0, 0)),
    ]
    ins = [proj, proj, proj, lr, gw, gb.reshape(1, -1), jnp.asarray(ms, bf16), jnp.asarray(lvl)]
    if final:
        o_other, nw = final_args
        in_specs += [
            pl.BlockSpec((rb, hps * dv), lambda b, h, s: (rblk(b, s), r_off + h)),
            pl.BlockSpec((rb, hps * dv), lambda b, h, s: (rblk(b, s), h)),
            pl.BlockSpec((1, dv), lambda b, h, s: (0, 0)),
        ]
        ins += [proj, o_other, nw.reshape(1, dv)]
    return pl.pallas_call(
        functools.partial(_gla_kernel, reverse=reverse, cps=cps, hps=hps, final=final, dk=dk),
        grid=(bsz, groups, nsteps),
        in_specs=in_specs,
        out_specs=pl.BlockSpec((rb, hps * dv), lambda b, h, s: (rblk(b, s), h)),
        out_shape=jax.ShapeDtypeStruct((t, heads * dv), bf16 if final else f32),
        scratch_shapes=[pltpu.VMEM((hps, dv, dk), f32)],
        compiler_params=_params("parallel", "parallel", "arbitrary"),
        name="gla_final" if final else "gla_partial",
    )(*ins)


N_VARIANTS = WIN_H


def _na_bias_tables(rpb, grid_rows):
    del grid_rows
    n_layers, heads = rpb.shape[0], rpb.shape[1]
    cols = np.arange(GRID_W)
    col_start = np.clip(cols - WIN_W // 2, 0, GRID_W - WIN_W)
    kc = np.arange(GRID_W)
    in_win = (kc[None, :] >= col_start[:, None]) & (kc[None, :] < col_start[:, None] + WIN_W)
    dc = np.clip(kc[None, :] - cols[:, None] + (WIN_W - 1), 0, 2 * WIN_W - 2)
    rpb_c = jnp.where(in_win[None, None, None], rpb[:, :, :, dc], NEG)
    tbl = jnp.stack([rpb_c[:, :, vi:vi + WIN_H] for vi in range(N_VARIANTS)], axis=2)
    tbl = tbl.reshape(n_layers, heads // 2, 2, N_VARIANTS, WIN_H, GRID_W, GRID_W)
    tbl = jnp.transpose(tbl, (0, 1, 3, 2, 5, 4, 6))
    return tbl.reshape(n_layers, heads // 2, N_VARIANTS, 2 * GRID_W, WIN_H * GRID_W)


def _na_kernel(q_ref, k_ref, v_ref, bias_ref, mb_ref, o_ref, *, bps, nsteps, grid_rows):
    lane = lax.broadcasted_iota(jnp.int32, (1, 2 * NA_HEAD_DIM), 1)
    head0 = lane < NA_HEAD_DIM
    km = k_ref[PAD:FRONT, :]
    vm = v_ref[PAD:FRONT, :]
    mb = mb_ref[0]
    mbias = jnp.concatenate([jnp.broadcast_to(mb[0:1], (GRID_W, N_META)),
                             jnp.broadcast_to(mb[1:2], (GRID_W, N_META))], axis=0)
    zero = jnp.zeros((), bf16)
    scale = NA_HEAD_DIM ** -0.5

    def stacked_queries(rows):
        q2 = q_ref[rows, :] * jnp.asarray(scale, bf16)
        return jnp.concatenate([jnp.where(head0, q2, zero), jnp.where(head0, zero, q2)], axis=0)

    def unstack(o):
        return jnp.where(head0, o[0:GRID_W], o[GRID_W:2 * GRID_W])

    row = lax.broadcasted_iota(jnp.int32, (GRID_W, 1), 0)
    blocks = range(bps)

    def step_body(step, carry):
        qq, vv, sl, sm, front, rows = {}, {}, {}, {}, {}, {}
        for j in blocks:
            g = step * bps + j
            r = g - 1
            r0 = jnp.clip(r - WIN_H // 2, 0, grid_rows - WIN_H)
            vi = jnp.clip(r0 - r + WIN_H // 2 + 3, 0, N_VARIANTS - 1)
            ks = pl.multiple_of((r0 + 1) * GRID_W, GRID_W)
            kk = k_ref[pl.ds(ks, WIN_H * GRID_W), :]
            vv[j] = v_ref[pl.ds(ks, WIN_H * GRID_W), :]
            rows[j] = pl.ds(pl.multiple_of(g * GRID_W, GRID_W), GRID_W)
            qq[j] = stacked_queries(rows[j])
            front[j] = g == 0
            sl[j] = lax.dot_general(qq[j], kk, NT, preferred_element_type=f32) + bias_ref[0, vi]
            sm[j] = lax.dot_general(qq[j], km, NT, preferred_element_type=f32) + mbias
        p_loc, p_meta, den = {}, {}, {}
        for j in blocks:
            s_loc = sl[j] + jnp.where(front[j], NEG, 0.0) if j == 0 else sl[j]
            mx = jnp.maximum(jnp.max(s_loc, axis=-1, keepdims=True), jnp.max(sm[j], axis=-1, keepdims=True))
            p_loc[j] = jnp.exp(s_loc - mx)
            p_meta[j] = jnp.exp(sm[j] - mx)
            den[j] = jnp.sum(p_loc[j], axis=-1, keepdims=True) + jnp.sum(p_meta[j], axis=-1, keepdims=True)
        for j in blocks:
            o = (jnp.dot(p_loc[j].astype(bf16), vv[j], preferred_element_type=f32)
                 + jnp.dot(p_meta[j].astype(bf16), vm, preferred_element_type=f32))
            o = unstack(o / den[j])
            if j == 0:
                o = jnp.where((row >= PAD) | jnp.logical_not(front[j]), o, 0.0)
            o_ref[rows[j], :] = o.astype(o_ref.dtype)
        return carry

    lax.fori_loop(0, nsteps, step_body, 0)


def _neighborhood_attention(proj, bias_tbl, meta_bias, bsz, lp, q_col):
    t = proj.shape[0]
    n_pairs = bias_tbl.shape[0]
    width = 2 * NA_HEAD_DIM
    grid_rows = (lp - FRONT) // GRID_W
    assert grid_rows >= WIN_H
    n_blocks = lp // GRID_W
    bps = _tile(n_blocks, (3, 2, 1))
    nsteps = n_blocks // bps
    qb = q_col // width
    return pl.pallas_call(
        functools.partial(_na_kernel, bps=bps, nsteps=nsteps, grid_rows=grid_rows),
        grid=(bsz, n_pairs),
        in_specs=[
            pl.BlockSpec((lp, width), lambda b, p: (b, qb + p)),
            pl.BlockSpec((lp, width), lambda b, p: (b, qb + n_pairs + p)),
            pl.BlockSpec((lp, width), lambda b, p: (b, qb + 2 * n_pairs + p)),
            pl.BlockSpec((1,) + bias_tbl.shape[1:], lambda b, p: (p, 0, 0, 0)),
            pl.BlockSpec((1, 2, N_META), lambda b, p: (p, 0, 0)),
        ],
        out_specs=pl.BlockSpec((lp, width), lambda b, p: (b, p)),
        out_shape=jax.ShapeDtypeStruct((t, n_pairs * width), bf16),
        compiler_params=_params("parallel", "parallel"),
        name="neighborhood_attention",
    )(proj, proj, proj, bias_tbl, meta_bias.reshape(n_pairs, 2, N_META))


ISSUE_UNROLL = 8


def _start_row_gather(src_hbm, idx_smem, slot, dst, sem, n, ns):
    base = slot * n

    def body(j, c):
        r0 = j * ISSUE_UNROLL
        for u in range(ISSUE_UNROLL):
            src_row = pl.multiple_of(idx_smem[base + r0 + u] * ns, ns)
            dst_row = pl.multiple_of((r0 + u) * ns, ns)
            pltpu.make_async_copy(src_hbm.at[pl.ds(src_row, ns), :], dst.at[slot, pl.ds(dst_row, ns), :],
                                  sem.at[slot]).start()
        return c

    lax.fori_loop(0, n // ISSUE_UNROLL, body, 0)


def _wait_row_gather(src_hbm, slot, dst, sem, n, ns):
    pltpu.make_async_copy(src_hbm.at[pl.ds(0, n * ns), :], dst.at[slot], sem.at[slot]).wait()


def _gather_pipeline(idx_hbm, src_hbm, idx_smem, buf, sem_idx, sem_rows, n, ns):
    g = pl.program_id(0)
    last = pl.num_programs(0) - 1
    slot = g % 2
    nxt = 1 - slot

    def idx_copy(step, s):
        return pltpu.make_async_copy(idx_hbm.at[step], idx_smem.at[pl.ds(s * n, n)], sem_idx.at[s])

    @pl.when(g == 0)
    def _():
        c = idx_copy(0, 0)
        c.start()
        c.wait()
        _start_row_gather(src_hbm, idx_smem, 0, buf, sem_rows, n, ns)

        @pl.when(last >= 1)
        def _():
            idx_copy(1, 1).start()

    @pl.when(g < last)
    def _():
        idx_copy(g + 1, nxt).wait()
        _start_row_gather(src_hbm, idx_smem, nxt, buf, sem_rows, n, ns)

    @pl.when(g + 1 < last)
    def _():
        idx_copy(g + 2, slot).start()

    _wait_row_gather(src_hbm, slot, buf, sem_rows, n, ns)
    return slot


def _expert_kernel(ge_ref, nu_ref, idx_hbm, x_hbm, w1_ref, b1_ref, w2_ref, b2_ref, rw_ref, y_ref,
                   idx_smem, xbuf, sem_idx, sem_rows):
    del ge_ref
    ns = xbuf.shape[1] // GROUP_ROWS
    slot = _gather_pipeline(idx_hbm, x_hbm, idx_smem, xbuf, sem_idx, sem_rows, GROUP_ROWS, ns)
    g = pl.program_id(0)
    ff = w2_ref.shape[2]

    @pl.when(g < nu_ref[0])
    def _():
        x = jnp.concatenate([_slab_column(xbuf, (slot,), 0, GROUP_ROWS, ns, c) for c in range(ns)], axis=1).astype(bf16)
        h = jnp.dot(x, w1_ref[0, 0], preferred_element_type=f32) + b1_ref[0]
        x_glu = jnp.minimum(h[:, :ff], SWIGLU_LIMIT)
        x_lin = jnp.clip(h[:, ff:], -SWIGLU_LIMIT, SWIGLU_LIMIT)
        act = x_glu * jax.nn.sigmoid(SWIGLU_ALPHA * x_glu) * (x_lin + 1.0)
        y = jnp.dot(act.astype(bf16), w2_ref[0, 0], preferred_element_type=f32) + b2_ref[0]
        _rows_to_slabs(y_ref, y * rw_ref[...])

    @pl.when(g >= nu_ref[0])
    def _():
        y_ref[...] = jnp.zeros_like(y_ref)


def _expert_groups(hs, row_tok, row_w, group_e, n_used, w1, b1, w2, b2, layer):
    _, n_exp, d, f2 = w1.shape
    nslab = d // LANES
    n_groups = row_tok.shape[0]
    ff = f2 // 2
    grid_spec = pltpu.PrefetchScalarGridSpec(
        num_scalar_prefetch=2,
        grid=(n_groups,),
        in_specs=[
            pl.BlockSpec(memory_space=pl.ANY),
            pl.BlockSpec(memory_space=pl.ANY),
            pl.BlockSpec((1, 1, d, f2), lambda g, ge, nu: (layer, ge[g], 0, 0)),
            pl.BlockSpec((1, 1, f2), lambda g, ge, nu: (ge[g], 0, 0)),
            pl.BlockSpec((1, 1, ff, d), lambda g, ge, nu: (layer, ge[g], 0, 0)),
            pl.BlockSpec((1, 1, d), lambda g, ge, nu: (ge[g], 0, 0)),
            pl.BlockSpec((GROUP_ROWS, 1), lambda g, ge, nu: (g, 0)),
        ],
        out_specs=pl.BlockSpec((GROUP_ROWS * nslab, LANES), lambda g, ge, nu: (g, 0)),
        scratch_shapes=[
            pltpu.SMEM((2 * GROUP_ROWS,), jnp.int32),
            pltpu.VMEM((2, GROUP_ROWS * nslab, LANES), f32),
            pltpu.SemaphoreType.DMA((2,)),
            pltpu.SemaphoreType.DMA((2,)),
        ],
    )
    return pl.pallas_call(
        _expert_kernel, grid_spec=grid_spec,
        out_shape=jax.ShapeDtypeStruct((n_groups * GROUP_ROWS * nslab, LANES), f32),
        compiler_params=_params("arbitrary"), name="expert_groups",
    )(group_e, n_used, row_tok, hs, w1, b1.reshape(n_exp, 1, f2), w2, b2.reshape(n_exp, 1, d),
      row_w.reshape(n_groups * GROUP_ROWS, 1))


COMBINE_TOKENS = 256


def _combine_ln_kernel(*refs, alpha, has_proj):
    it = iter(refs)
    idx_hbm, y_hbm, res_ref, w_ref, b_ref = (next(it) for _ in range(5))
    pw_ref, pb_ref = (next(it), next(it)) if has_proj else (None, None)
    h_ref, hb_ref = next(it), next(it)
    p_ref = next(it) if has_proj else None
    idx_smem, buf, sem_idx, sem_rows = (next(it) for _ in range(4))
    tc = COMBINE_TOKENS
    ns = res_ref.shape[0] // tc
    slot = _gather_pipeline(idx_hbm, y_hbm, idx_smem, buf, sem_idx, sem_rows, TOP_K * tc, ns)
    cols = []
    for c in range(ns):
        acc = _slab_column(buf, (slot,), 0, tc, ns, c)
        for kk in range(1, TOP_K):
            acc = acc + _slab_column(buf, (slot,), kk * tc, tc, ns, c)
        cols.append(alpha * _slab_column(res_ref, (), 0, tc, ns, c) + acc)
    h = _ln_math(jnp.concatenate(cols, axis=1), w_ref[...], b_ref[...])
    h_ref[...] = h
    hb_ref[...] = h.astype(bf16)
    if has_proj:
        p_ref[...] = _small_proj(h, pw_ref, pb_ref)


def _combine_layer_norm(y_rows, dest, hs, alpha, w, b, pw=None, pb=None):
    d = w.shape[0]
    nslab = d // LANES
    t = hs.shape[0] // nslab
    tc = COMBINE_TOKENS
    n_tiles = t // tc
    has_proj = pw is not None
    idx = dest.reshape(n_tiles, tc, TOP_K).transpose(0, 2, 1).reshape(n_tiles, TOP_K * tc)
    row = pl.BlockSpec((tc, d), lambda i: (i, 0))
    vec = pl.BlockSpec((1, d), lambda i: (0, 0))
    ins = [idx, y_rows, hs, w.reshape(1, d), b.reshape(1, d)]
    in_specs = [pl.BlockSpec(memory_space=pl.ANY), pl.BlockSpec(memory_space=pl.ANY),
                pl.BlockSpec((tc * nslab, LANES), lambda i: (i, 0)), vec, vec]
    outs = [jax.ShapeDtypeStruct((t, d), f32), jax.ShapeDtypeStruct((t, d), bf16)]
    out_specs = [row, row]
    if has_proj:
        ins += [pw, pb.reshape(1, SMALL_N)]
        in_specs += [pl.BlockSpec((3 * d, SMALL_N), lambda i: (0, 0)), pl.BlockSpec((1, SMALL_N), lambda i: (0, 0))]
        outs.append(jax.ShapeDtypeStruct((t, SMALL_N), f32))
        out_specs.append(pl.BlockSpec((tc, SMALL_N), lambda i: (i, 0)))
    return pl.pallas_call(
        functools.partial(_combine_ln_kernel, alpha=alpha, has_proj=has_proj),
        grid=(n_tiles,), in_specs=in_specs, out_specs=out_specs, out_shape=outs,
        scratch_shapes=[
            pltpu.SMEM((2 * TOP_K * tc,), jnp.int32),
            pltpu.VMEM((2, TOP_K * tc * nslab, LANES), f32),
            pltpu.SemaphoreType.DMA((2,)),
            pltpu.SemaphoreType.DMA((2,)),
        ],
        compiler_params=_params("arbitrary"), name="combine_layer_norm",
    )(*ins)


def _route(top_e, gate, n_exp):
    t = top_e.shape[0]
    n = t * TOP_K
    flat_e = top_e.reshape(-1)
    pos = jnp.arange(n, dtype=jnp.int32)
    e_sorted, order = lax.sort((flat_e, pos), num_keys=1, is_stable=True)
    _, inv = lax.sort((order, pos), num_keys=1)
    bounds = jnp.searchsorted(e_sorted, jnp.arange(n_exp + 1, dtype=jnp.int32), side="left",
                              method="scan_unrolled").astype(jnp.int32)
    start = bounds[:n_exp]
    counts = bounds[1:] - start
    padded = (counts + GROUP_ROWS - 1) // GROUP_ROWS * GROUP_ROWS
    pend = jnp.cumsum(padded)
    pstart = pend - padded
    dest = (inv + (pstart - start)[flat_e]).reshape(t, TOP_K)
    n_groups = -(-n // GROUP_ROWS) + n_exp
    group_row0 = jnp.arange(n_groups, dtype=jnp.int32) * GROUP_ROWS
    group_e = jnp.minimum(jnp.sum(pend[None, :] <= group_row0[:, None], axis=1, dtype=jnp.int32), n_exp - 1)
    within = (group_row0 - pstart[group_e])[:, None] + jnp.arange(GROUP_ROWS, dtype=jnp.int32)[None, :]
    live = (within < counts[group_e][:, None]) & (group_row0 < pend[n_exp - 1])[:, None]
    src = jnp.clip(start[group_e][:, None] + within, 0, n - 1)
    pair = order[src]
    row_tok = jnp.where(live, pair // TOP_K, 0).astype(jnp.int32)
    row_w = jnp.where(live, gate.reshape(-1)[pair], 0.0)
    n_used = (pend[n_exp - 1] // GROUP_ROWS).astype(jnp.int32).reshape(1)
    return row_tok, row_w, group_e, n_used, dest


def kernel(x, meta_tokens, ln0_w, ln0_b, w_in, gk_fwd_w, gk_fwd_b, gk_bwd_w, gk_bwd_b, gla_norm_w, w_branch_a, rpb, meta_bias, w_branch_b, w_out, ln1_w, ln1_b, router_w, router_b, w1, b1, w2, b2, ln2_w, ln2_b):
    bsz, seq, d = x.shape
    depth = w_in.shape[0]
    alpha = (2 * depth) ** 0.25
    lp = FRONT + seq
    kdim = gk_fwd_w.shape[2]
    vdim = w_branch_a.shape[1]
    ndim = w_branch_b.shape[1]
    n_exp = router_w.shape[2]
    ff = w2.shape[2]
    assert seq % GRID_W == 0 and n_exp <= SMALL_N and 2 * GATE_RANK <= SMALL_N

    lr_col = 2 * kdim + 2 * vdim
    nq_col = lr_col
    ga_col = nq_col + 3 * ndim
    gb_col = ga_col + d
    w_main = jnp.concatenate([w_in[:, :, :lr_col], w_in[:, :, lr_col + 2 * GATE_RANK:]], axis=2).astype(bf16)
    w_lr = _split_rhs(jnp.pad(w_in[:, :, lr_col:lr_col + 2 * GATE_RANK], ((0, 0), (0, 0), (0, SMALL_N - 2 * GATE_RANK))))
    w_rt = _split_rhs(jnp.pad(router_w, ((0, 0), (0, 0), (0, SMALL_N - n_exp))))
    b_rt = jnp.pad(router_b, ((0, 0), (0, SMALL_N - n_exp)))
    zeros_small = jnp.zeros((SMALL_N,), f32)
    w_a = w_branch_a.astype(bf16)
    w_b = w_branch_b.astype(bf16)
    w_o = w_out.astype(bf16)
    w1p = _deinterleave_cast(w1.reshape(depth * n_exp, d, 2 * ff)).reshape(depth, n_exp, d, 2 * ff)
    b1p = jnp.concatenate([b1[..., 0::2], b1[..., 1::2]], axis=-1)
    w2b = w2.astype(bf16)
    bias_tbl = _na_bias_tables(rpb, seq // GRID_W)
    gw_f, gw_b = _split_rhs(gk_fwd_w), _split_rhs(gk_bwd_w)

    front = jnp.concatenate([jnp.zeros((PAD, d), x.dtype), meta_tokens.astype(x.dtype)], axis=0)
    xp = jnp.concatenate([jnp.broadcast_to(front[None], (bsz, FRONT, d)), x], axis=1).reshape(bsz * lp, d)
    h, hb, small = _embed_layer_norm(xp, ln0_w, ln0_b, w_lr[0], zeros_small)

    for l in range(depth):
        proj = _matmul(hb, w_main, l, bf16, "in_proj")
        lr_f = _split_lhs(small[:, :GATE_RANK])
        lr_b = _split_lhs(small[:, GATE_RANK:2 * GATE_RANK])
        o_bwd = _gla_direction(proj, lr_b, gw_b[l], gk_bwd_b[l], bsz, lp, reverse=True)
        o_a = _gla_direction(proj, lr_f, gw_f[l], gk_fwd_b[l], bsz, lp, reverse=False,
                             final_args=(o_bwd, gla_norm_w[l]))
        o_b = _neighborhood_attention(proj, bias_tbl[l], meta_bias[l], bsz, lp, nq_col)
        mixed = _branch_merge(o_a, o_b, w_a[l], w_b[l], proj, ga_col, gb_col)
        hs, top_e, gate = _mixer_layer_norm(mixed, w_o[l], h, alpha, ln1_w[l], ln1_b[l], w_rt[l], b_rt[l], n_exp)

        row_tok, row_w, group_e, n_used, dest = _route(top_e[:, :TOP_K], gate[:, :TOP_K], n_exp)
        y_rows = _expert_groups(hs, row_tok, row_w, group_e, n_used, w1p, b1p[l], w2b, b2[l], l)
        if l + 1 < depth:
            h, hb, small = _combine_layer_norm(y_rows, dest, hs, alpha, ln2_w[l], ln2_b[l], w_lr[l + 1], zeros_small)
        else:
            h, hb = _combine_layer_norm(y_rows, dest, hs, alpha, ln2_w[l], ln2_b[l])
    del ff, vdim
    return h.reshape(bsz, lp, d)[:, FRONT:]
```
